```python
import math
import jax, jax.numpy as jnp
from jax import lax
import numpy as np

D_MODEL = 2048
BATCH = 4
SEQ = 4096
DEPTH = 4

N_META = 16
D_FF = 5632
SSM_WIDTH = D_MODEL // 2
SSM_GROUP = 16
SSM_GROUPS = SSM_WIDTH // SSM_GROUP
SSM_STATE = 64
DT_MIN = 1e-3
DT_MAX = 1e-1
RWKV_WIDTH = D_MODEL // 2
RWKV_HEAD = 64
RWKV_HEADS = RWKV_WIDTH // RWKV_HEAD
LORA_DECAY = 64
LORA_ICLR = 64
LORA_VRES = 32
LORA_GATE = 160
NORM_EPS = 1e-6
LNX_EPS = RWKV_HEAD * 1e-5

COL_U = 0
COL_R = COL_U + SSM_WIDTH
COL_K = COL_R + RWKV_WIDTH
COL_V = COL_K + RWKV_WIDTH
COL_W = COL_V + RWKV_WIDTH
COL_A = COL_W + LORA_DECAY
COL_G = COL_A + LORA_ICLR
COL_GATE_A = COL_G + LORA_GATE
COL_GATE_B = COL_GATE_A + D_MODEL
P_COMMON = COL_GATE_B + D_MODEL
P_REST = P_COMMON + LORA_VRES
SHIFT_LO = COL_R
SHIFT_HI = COL_GATE_A

kernel_name = "hybrid_s5_rwkv7_macaron_meta"


def rms_norm(x, g):
    xf = x.astype(jnp.float32)
    inv = lax.rsqrt(jnp.mean(xf * xf, axis=-1, keepdims=True) + NORM_EPS)
    return (xf * inv).astype(x.dtype) * g


def swiglu(h, w_gate, w_up, w_down):
    return (jax.nn.silu(h @ w_gate) * (h @ w_up)) @ w_down


def token_shift(p):
    return jnp.pad(p, ((0, 0), (1, 0), (0, 0)))[:, :-1]


def cmul(ar, ai, br, bi):
    return ar * br - ai * bi, ar * bi + ai * br


def s5_branch(u, lam_re, lam_im, log_dt, b_re, b_im, c_re, c_im, d_skip, w_glu):
    f32 = jnp.float32
    bsz, t_len, _ = u.shape
    uf = u.astype(f32).reshape(bsz, t_len, SSM_GROUPS, SSM_GROUP)
    lr = lam_re.astype(f32)
    li = lam_im.astype(f32)
    dt = jnp.exp(log_dt.astype(f32))[:, None]
    mag = jnp.exp(lr * dt)
    abar_re = mag * jnp.cos(li * dt)
    abar_im = mag * jnp.sin(li * dt)
    den = lr * lr + li * li
    nr = abar_re - 1.0
    ni = abar_im
    q_re = (nr * lr + ni * li) / den
    q_im = (ni * lr - nr * li) / den
    bb_re, bb_im = cmul(q_re[..., None], q_im[..., None], b_re.astype(f32), b_im.astype(f32))
    drive_re = jnp.einsum("btgc,gnc->btgn", uf, bb_re)
    drive_im = jnp.einsum("btgc,gnc->btgn", uf, bb_im)
    a_re = jnp.broadcast_to(abar_re, (1, t_len, SSM_GROUPS, SSM_STATE))
    a_im = jnp.broadcast_to(abar_im, (1, t_len, SSM_GROUPS, SSM_STATE))

    def combine(e1, e2):
        a1r, a1i, b1r, b1i = e1
        a2r, a2i, b2r, b2i = e2
        ar, ai = cmul(a2r, a2i, a1r, a1i)
        br, bi = cmul(a2r, a2i, b1r, b1i)
        return ar, ai, br + b2r, bi + b2i

    _, _, s_re, s_im = lax.associative_scan(combine, (a_re, a_im, drive_re, drive_im), axis=1)
    y = (jnp.einsum("btgn,gcn->btgc", s_re, c_re.astype(f32))
         - jnp.einsum("btgn,gcn->btgc", s_im, c_im.astype(f32)))
    y = y.reshape(bsz, t_len, SSM_WIDTH) + d_skip.astype(f32) * uf.reshape(bsz, t_len, SSM_WIDTH)
    y = jax.nn.gelu(y).astype(u.dtype)
    return y * jax.nn.sigmoid(y @ w_glu)


def _heads(t):
    return t.reshape(t.shape[0], t.shape[1], RWKV_HEADS, RWKV_HEAD)


def rwkv7_recurrence(r, w, k, v, a, b):
    bsz = r.shape[0]

    def step(s, inp):
        r_t, w_t, k_t, v_t, a_t, b_t = inp
        sa = jnp.einsum("bhvk,bhk->bhv", s, a_t)
        s = s * w_t[:, :, None, :] + sa[..., None] * b_t[:, :, None, :] + v_t[..., None] * k_t[:, :, None, :]
        return s, jnp.einsum("bhvk,bhk->bhv", s, r_t)

    s0 = jnp.zeros((bsz, RWKV_HEADS, RWKV_HEAD, RWKV_HEAD), jnp.float32)
    xs = tuple(jnp.moveaxis(t, 1, 0) for t in (r, w, k, v, a, b))
    _, out = lax.scan(step, s0, xs)
    return jnp.moveaxis(out, 0, 1)


def rwkv7_branch(r, k, v, xw, xa, xg, w0, w2, a0, a2, g2, k_k, k_a, r_k, lnx_w, lnx_b):
    f32 = jnp.float32
    bsz, t_len, _ = r.shape
    w_log = -jax.nn.softplus(-(w0 + jnp.tanh(xw) @ w2)) - 0.5
    decay = jnp.exp(-jnp.exp(w_log.astype(f32)))
    iclr = jax.nn.sigmoid(a0 + xa @ a2)
    gate = jax.nn.sigmoid(xg) @ g2
    kk = _heads(k * k_k).astype(f32)
    kk = kk * lax.rsqrt(jnp.sum(kk * kk, axis=-1, keepdims=True) + 1e-12)
    k = k * (1.0 + (iclr - 1.0) * k_a)
    r_h = _heads(r).astype(f32)
    k_h = _heads(k).astype(f32)
    v_h = _heads(v).astype(f32)
    iclr_h = _heads(iclr).astype(f32)
    o = rwkv7_recurrence(r_h, _heads(decay), k_h, v_h, -kk, kk * iclr_h)
    mean = jnp.mean(o, axis=-1, keepdims=True)
    var = jnp.mean(jnp.square(o - mean), axis=-1, keepdims=True)
    o = (o - mean) * lax.rsqrt(var + LNX_EPS)
    bonus = jnp.sum(r_h * k_h * r_k.astype(f32), axis=-1, keepdims=True) * v_h
    o = (o.reshape(bsz, t_len, RWKV_WIDTH).astype(r.dtype) * lnx_w + lnx_b
         + bonus.reshape(bsz, t_len, RWKV_WIDTH).astype(r.dtype))
    return o * gate


def setup_inputs(seed: int = 0) -> dict:
    key = jax.random.key(seed)
    ks = iter(jax.random.split(key, 64))
    f32 = jnp.float32
    L = DEPTH
    Lr = DEPTH - 1
    G, N, C = SSM_GROUPS, SSM_STATE, SSM_GROUP

    def nrm(shape, scale):
        return scale * jax.random.normal(next(ks), shape, f32)

    def gain(shape):
        return 1.0 + nrm(shape, 0.02)

    def unif(shape, lo, hi):
        return jax.random.uniform(next(ks), shape, f32, lo, hi)

    n_idx = jnp.arange(SSM_STATE, dtype=f32)
    d_in = D_MODEL ** -0.5
    return {
        "x": nrm((BATCH, SEQ, D_MODEL), 1.0),
        "meta_tokens": nrm((N_META, D_MODEL), 1.0),
        "ffn1_norm": gain((L, D_MODEL)),
        "ffn1_w_gate": nrm((L, D_MODEL, D_FF), d_in),
        "ffn1_w_up": nrm((L, D_MODEL, D_FF), d_in),
        "ffn1_w_down": nrm((L, D_FF, D_MODEL), D_FF ** -0.5),
        "mix_norm": gain((L, D_MODEL)),
        "w_in_first": nrm((D_MODEL, P_COMMON), d_in),
        "w_in_rest": nrm((Lr, D_MODEL, P_REST), d_in),
        "mu_shift": unif((L, SHIFT_HI - SHIFT_LO), 0.0, 1.0),
        "mu_vres": unif((Lr, LORA_VRES), 0.0, 1.0),
        "ssm_lambda_re": -0.5 + nrm((L, G, N), 0.01),
        "ssm_lambda_im": math.pi * n_idx + nrm((L, G, N), 0.01),
        "ssm_log_dt": unif((L, G), math.log(DT_MIN), math.log(DT_MAX)),
        "ssm_b_re": nrm((L, G, N, C), (2.0 * C) ** -0.5),
        "ssm_b_im": nrm((L, G, N, C), (2.0 * C) ** -0.5),
        "ssm_c_re": nrm((L, G, C, N), N ** -0.5),
        "ssm_c_im": nrm((L, G, C, N), N ** -0.5),
        "ssm_d": nrm((L, SSM_WIDTH), 0.5),
        "ssm_w_glu": nrm((L, SSM_WIDTH, SSM_WIDTH), SSM_WIDTH ** -0.5),
        "rwkv_w0": unif((L, RWKV_WIDTH), -6.0, -1.0),
        "rwkv_w2": nrm((L, LORA_DECAY, RWKV_WIDTH), 0.5 * LORA_DECAY ** -0.5),
        "rwkv_a0": nrm((L, RWKV_WIDTH), 0.1),
        "rwkv_a2": nrm((L, LORA_ICLR, RWKV_WIDTH), 0.5 * LORA_ICLR ** -0.5),
        "rwkv_v0": nrm((Lr, RWKV_WIDTH), 0.1),
        "rwkv_v2": nrm((Lr, LORA_VRES, RWKV_WIDTH), 0.5 * LORA_VRES ** -0.5),
        "rwkv_g2": nrm((L, LORA_GATE, RWKV_WIDTH), LORA_GATE ** -0.5),
        "rwkv_k_k": 0.85 + nrm((L, RWKV_WIDTH), 0.02),
        "rwkv_k_a": gain((L, RWKV_WIDTH)),
        "rwkv_r_k": nrm((L, RWKV_HEADS, RWKV_HEAD), 0.1),
        "rwkv_lnx_w": gain((L, RWKV_WIDTH)),
        "rwkv_lnx_b": nrm((L, RWKV_WIDTH), 0.01),
        "w_up_ssm": nrm((L, SSM_WIDTH, D_MODEL), SSM_WIDTH ** -0.5),
        "w_up_rwkv": nrm((L, RWKV_WIDTH, D_MODEL), RWKV_WIDTH ** -0.5),
        "w_out": nrm((L, D_MODEL, D_MODEL), d_in),
        "ffn2_norm": gain((L, D_MODEL)),
        "ffn2_w_gate": nrm((L, D_MODEL, D_FF), d_in),
        "ffn2_w_up": nrm((L, D_MODEL, D_FF), d_in),
        "ffn2_w_down": nrm((L, D_FF, D_MODEL), D_FF ** -0.5),
        "final_norm": gain((D_MODEL,)),
    }


def reference(x, meta_tokens, ffn1_norm, ffn1_w_gate, ffn1_w_up, ffn1_w_down, mix_norm,
              w_in_first, w_in_rest, mu_shift, mu_vres,
              ssm_lambda_re, ssm_lambda_im, ssm_log_dt, ssm_b_re, ssm_b_im, ssm_c_re, ssm_c_im,
              ssm_d, ssm_w_glu,
              rwkv_w0, rwkv_w2, rwkv_a0, rwkv_a2, rwkv_v0, rwkv_v2, rwkv_g2,
              rwkv_k_k, rwkv_k_a, rwkv_r_k, rwkv_lnx_w, rwkv_lnx_b,
              w_up_ssm, w_up_rwkv, w_out,
              ffn2_norm, ffn2_w_gate, ffn2_w_up, ffn2_w_down, final_norm):
    bsz = x.shape[0]
    meta = jnp.broadcast_to(meta_tokens[None].astype(x.dtype), (bsz, N_META, D_MODEL))
    h_res = jnp.concatenate([meta, x], axis=1)
    v_first = None
    for i in range(DEPTH):
        h = rms_norm(h_res, ffn1_norm[i])
        h_res = h_res + 0.5 * swiglu(h, ffn1_w_gate[i], ffn1_w_up[i], ffn1_w_down[i])

        h = rms_norm(h_res, mix_norm[i])
        p = h @ (w_in_first if i == 0 else w_in_rest[i - 1])
        u = p[..., COL_U:COL_R]
        p_rw = p[..., SHIFT_LO:SHIFT_HI]
        p_rw = p_rw + mu_shift[i] * (token_shift(p_rw) - p_rw)
        xr = p_rw[..., COL_R - SHIFT_LO:COL_K - SHIFT_LO]
        xk = p_rw[..., COL_K - SHIFT_LO:COL_V - SHIFT_LO]
        xv = p_rw[..., COL_V - SHIFT_LO:COL_W - SHIFT_LO]
        xw = p_rw[..., COL_W - SHIFT_LO:COL_A - SHIFT_LO]
        xa = p_rw[..., COL_A - SHIFT_LO:COL_G - SHIFT_LO]
        xg = p_rw[..., COL_G - SHIFT_LO:COL_GATE_A - SHIFT_LO]
        if i == 0:
            v_first = xv
            v = xv
        else:
            xvr = p[..., P_COMMON:P_REST]
            xvr = xvr + mu_vres[i - 1] * (token_shift(xvr) - xvr)
            v = xv + (v_first - xv) * jax.nn.sigmoid(rwkv_v0[i - 1] + xvr @ rwkv_v2[i - 1])

        y_a = s5_branch(u, ssm_lambda_re[i], ssm_lambda_im[i], ssm_log_dt[i], ssm_b_re[i], ssm_b_im[i],
                        ssm_c_re[i], ssm_c_im[i], ssm_d[i], ssm_w_glu[i])
        y_b = rwkv7_branch(xr, xk, v, xw, xa, xg, rwkv_w0[i], rwkv_w2[i], rwkv_a0[i], rwkv_a2[i],
                           rwkv_g2[i], rwkv_k_k[i], rwkv_k_a[i], rwkv_r_k[i], rwkv_lnx_w[i], rwkv_lnx_b[i])
        g_a = jax.nn.sigmoid(p[..., COL_GATE_A:COL_GATE_B])
        g_b = jax.nn.sigmoid(p[..., COL_GATE_B:P_COMMON])
        merged = g_a * (y_a @ w_up_ssm[i]) + g_b * (y_b @ w_up_rwkv[i])
        h_res = h_res + merged @ w_out[i]

        h = rms_norm(h_res, ffn2_norm[i])
        h_res = h_res + 0.5 * swiglu(h, ffn2_w_gate[i], ffn2_w_up[i], ffn2_w_down[i])
    out = rms_norm(h_res, final_norm)
    return out[:, N_META:]
```

```python
import functools
import math

import jax
import jax.numpy as jnp
from jax import lax
from jax.experimental import pallas as pl
from jax.experimental.pallas import tpu as pltpu

F32 = jnp.float32
BF16 = jnp.bfloat16
HIGHEST = lax.Precision.HIGHEST

NORM_EPS = 1e-6
RWKV_HEAD = 64
LNX_EPS = RWKV_HEAD * 1e-5
SSM_GROUP = 16
S5_CHUNK = 16
RWKV_CHUNK = 64
LORA_BLOCK = 512
SEQ_ALIGN = 128
VMEM_LIMIT_BYTES = 56 * 1024 * 1024


def _cparams(*sem):
    return pltpu.CompilerParams(dimension_semantics=sem, vmem_limit_bytes=VMEM_LIMIT_BYTES)


def _rms(x, g):
    inv = lax.rsqrt(jnp.mean(x * x, axis=-1, keepdims=True) + NORM_EPS)
    return (x * inv) * g


def _row_tile(rows):
    for tm in (512, 256, 128, 64, 32, 16, 8):
        if rows % tm == 0:
            return tm
    raise ValueError(rows)


def _col_tile(cols):
    for tn in (512, 256, 128):
        if cols % tn == 0:
            return tn
    raise ValueError(cols)


def _ffn_kernel(x_ref, g_ref, wg_ref, wu_ref, wd_ref, o_ref, h_sc):
    f = pl.program_id(1)

    @pl.when(f == 0)
    def _():
        x = x_ref[...]
        h_sc[...] = _rms(x, g_ref[...]).astype(BF16)
        o_ref[...] = x

    h = h_sc[...]
    a = jnp.dot(h, wg_ref[...], preferred_element_type=F32)
    b = jnp.dot(h, wu_ref[...], preferred_element_type=F32)
    z = (0.5 * (a * jax.nn.sigmoid(a)) * b).astype(BF16)
    o_ref[...] += jnp.dot(z, wd_ref[...], preferred_element_type=F32)


def _ffn(x, g, wg, wu, wd):
    rows, d = x.shape
    dff = wg.shape[1]
    tm = _row_tile(rows)
    tf = _col_tile(dff)
    return pl.pallas_call(
        _ffn_kernel,
        grid=(rows // tm, dff // tf),
        in_specs=[
            pl.BlockSpec((tm, d), lambda i, f: (i, 0)),
            pl.BlockSpec((1, d), lambda i, f: (0, 0)),
            pl.BlockSpec((d, tf), lambda i, f: (0, f)),
            pl.BlockSpec((d, tf), lambda i, f: (0, f)),
            pl.BlockSpec((tf, d), lambda i, f: (f, 0)),
        ],
        out_specs=pl.BlockSpec((tm, d), lambda i, f: (i, 0)),
        out_shape=jax.ShapeDtypeStruct((rows, d), F32),
        scratch_shapes=[pltpu.VMEM((tm, d), BF16)],
        compiler_params=_cparams("parallel", "arbitrary"),
        name="ffn",
    )(x, g, wg, wu, wd)


def _proj_kernel(x_ref, g_ref, w_ref, o_ref, h_sc):
    @pl.when(pl.program_id(1) == 0)
    def _():
        h_sc[...] = _rms(x_ref[...], g_ref[...]).astype(BF16)

    o_ref[...] = jnp.dot(h_sc[...], w_ref[...], preferred_element_type=F32)


def _proj(x, g, w):
    rows, d = x.shape
    n = w.shape[1]
    tm = _row_tile(rows)
    tn = _col_tile(n)
    return pl.pallas_call(
        _proj_kernel,
        grid=(rows // tm, n // tn),
        in_specs=[
            pl.BlockSpec((tm, d), lambda i, j: (i, 0)),
            pl.BlockSpec((1, d), lambda i, j: (0, 0)),
            pl.BlockSpec((d, tn), lambda i, j: (0, j)),
        ],
        out_specs=pl.BlockSpec((tm, tn), lambda i, j: (i, j)),
        out_shape=jax.ShapeDtypeStruct((rows, n), F32),
        scratch_shapes=[pltpu.VMEM((tm, d), BF16)],
        compiler_params=_cparams("parallel", "arbitrary"),
        name="in_proj",
    )(x, g, w)


def _cpow(lr, li, dt, m):
    mag = jnp.exp(lr * dt * m)
    ang = li * dt * m
    return mag * jnp.cos(ang), mag * jnp.sin(ang)


def _s5_kernel(x_ref, ldt_ref, lrc_ref, lic_ref, lrr_ref, lir_ref, ctre_ref, ctim_ref,
               btre_ref, btim_ref, bttre_ref, bttim_ref, o_ref, m_sc, *, n_chunks):
    lc = S5_CHUNK
    cw = SSM_GROUP
    width = lc * cw
    rows = x_ref.shape[0]
    n_state = lrc_ref.shape[0]
    dt = jnp.exp(ldt_ref[...])
    lr_c, li_c = lrc_ref[...], lic_ref[...]
    lr_r, li_r = lrr_ref[...], lir_ref[...]

    are, aim = _cpow(lr_r, li_r, dt, 1.0)
    den = lr_r * lr_r + li_r * li_r
    nr, ni = are - 1.0, aim
    q_re = (nr * lr_r + ni * li_r) / den
    q_im = (ni * lr_r - nr * li_r) / den

    lag = (lax.broadcasted_iota(jnp.int32, (1, width), 1) // cw).astype(F32)
    ct_re, ct_im = ctre_ref[...], ctim_ref[...]
    p_re, p_im = _cpow(lr_c, li_c, dt, lag)
    d0_re = ct_re * p_re - ct_im * p_im
    d0_im = ct_re * p_im + ct_im * p_re
    bt_re, bt_im = btre_ref[...], btim_ref[...]
    bbt_re = q_re * bt_re - q_im * bt_im
    bbt_im = q_re * bt_im + q_im * bt_re
    m0 = (jnp.dot(bbt_re, d0_re, preferred_element_type=F32, precision=HIGHEST)
          - jnp.dot(bbt_im, d0_im, preferred_element_type=F32, precision=HIGHEST))
    lane = lax.broadcasted_iota(jnp.int32, (cw, width), 1)
    for j in range(lc):
        blk = m0 if j == 0 else jnp.where(lane >= j * cw, pltpu.roll(m0, j * cw, axis=1), 0.0)
        m_sc[j * cw:(j + 1) * cw, :] = blk

    rowj = (lax.broadcasted_iota(jnp.int32, (width, 1), 0) // cw).astype(F32)
    e_pre, e_pim = _cpow(lr_r, li_r, dt, (lc - 1.0) - rowj)
    btt_re, btt_im = bttre_ref[...], bttim_ref[...]
    bb2_re = q_re * btt_re - q_im * btt_im
    bb2_im = q_re * btt_im + q_im * btt_re
    e_mat = jnp.concatenate([bb2_re * e_pre - bb2_im * e_pim,
                             bb2_re * e_pim + bb2_im * e_pre], axis=1)

    p1_re, p1_im = _cpow(lr_c, li_c, dt, lag + 1.0)
    p_mat = jnp.concatenate([ct_re * p1_re - ct_im * p1_im,
                             -(ct_re * p1_im + ct_im * p1_re)], axis=0)

    x = x_ref[...].astype(BF16)
    y = jnp.dot(x, m_sc[...].astype(BF16), preferred_element_type=F32)
    s = jnp.dot(x, e_mat.astype(BF16), preferred_element_type=F32)

    crow = lax.broadcasted_iota(jnp.int32, (rows, 1), 0) % n_chunks
    shift = 1
    while shift < n_chunks:
        pre, pim = _cpow(lr_r, li_r, dt, float(lc * shift))
        a1 = jnp.concatenate([pre, pre], axis=1)
        a2 = jnp.concatenate([-pim, pim], axis=1)
        prev = jnp.where(crow >= shift, pltpu.roll(s, shift, axis=0), 0.0)
        s = s + a1 * prev + a2 * pltpu.roll(prev, n_state, axis=1)
        shift *= 2
    s_start = jnp.where(crow >= 1, pltpu.roll(s, 1, axis=0), 0.0)
    y = y + jnp.dot(s_start.astype(BF16), p_mat.astype(BF16), preferred_element_type=F32)
    o_ref[...] = y


def _s5_scan(xt, ldt, lr, li, c_re, c_im, b_re, b_im, n_chunks):
    g, rows, width = xt.shape
    n = lr.shape[1]
    cw = SSM_GROUP
    lc = S5_CHUNK
    ct_re = jnp.tile(jnp.swapaxes(c_re, 1, 2), (1, 1, lc))
    ct_im = jnp.tile(jnp.swapaxes(c_im, 1, 2), (1, 1, lc))
    bt_re = jnp.swapaxes(b_re, 1, 2)
    bt_im = jnp.swapaxes(b_im, 1, 2)
    btt_re = jnp.tile(bt_re, (1, lc, 1))
    btt_im = jnp.tile(bt_im, (1, lc, 1))
    g3 = lambda a, b: pl.BlockSpec((None, a, b), lambda i: (i, 0, 0))
    return pl.pallas_call(
        functools.partial(_s5_kernel, n_chunks=n_chunks),
        grid=(g,),
        in_specs=[g3(rows, width), g3(1, 1), g3(n, 1), g3(n, 1), g3(1, n), g3(1, n),
                  g3(n, width), g3(n, width), g3(cw, n), g3(cw, n), g3(width, n), g3(width, n)],
        out_specs=g3(rows, width),
        out_shape=jax.ShapeDtypeStruct((g, rows, width), F32),
        scratch_shapes=[pltpu.VMEM((width, width), F32)],
        compiler_params=_cparams("parallel"),
        name="s5_scan",
    )(xt, ldt.reshape(g, 1, 1), lr.reshape(g, n, 1), li.reshape(g, n, 1), lr.reshape(g, 1, n),
      li.reshape(g, 1, n), ct_re, ct_im, bt_re, bt_im, btt_re, btt_im)


(_V_W0, _V_A0, _V_V0, _V_KK, _V_KA, _V_RK, _V_LNW, _V_LNB, _V_MUR, _V_MUK, _V_MUV) = range(11)
_VEC_ROWS = 16


def _softplus(z):
    return jnp.maximum(z, 0.0) + jnp.log(1.0 + jnp.exp(-jnp.abs(z)))


def _dot_nt(a, b, **kw):
    return lax.dot_general(a, b, (((1,), (1,)), ((), ())), preferred_element_type=F32, **kw)


def _dot_tn(a, b, **kw):
    return lax.dot_general(a, b, (((0,), (0,)), ((), ())), preferred_element_type=F32, **kw)


def _rwkv_head(rt, at, bt, kt, v, bh, kh, wl, s0, strict, incl, prec):
    lc = rt.shape[0]
    dot = functools.partial(jnp.dot, preferred_element_type=F32, precision=prec)
    gm = _dot_nt(jnp.concatenate([at, rt], axis=0), jnp.concatenate([bt, kt], axis=0),
                 precision=prec)
    a_ab = jnp.where(strict, gm[:lc, :lc], 0.0)
    a_ak = jnp.where(strict, gm[:lc, lc:], 0.0)
    r_b = jnp.where(incl, gm[lc:, :lc], 0.0)
    r_k = jnp.where(incl, gm[lc:, lc:], 0.0)
    y = jnp.concatenate([at, dot(a_ak, v)], axis=1)
    nk = a_ab
    steps = max(1, (lc - 1).bit_length())
    for i in range(steps):
        y = y + dot(nk, y)
        if i + 1 < steps:
            nk = dot(nk, nk)
    x1, x2 = y[:, :RWKV_HEAD], y[:, RWKV_HEAD:]
    z1 = rt + dot(r_b, x1)
    z2 = dot(r_b, x2) + dot(r_k, v)
    uo = _dot_nt(jnp.concatenate([x1, z1], axis=0), s0, precision=prec)
    u = uo[:lc] + x2
    o = uo[lc:] + z2
    s_new = s0 * wl + _dot_tn(jnp.concatenate([u, v], axis=0),
                              jnp.concatenate([bh, kh], axis=0), precision=prec)
    return o, s_new


def _rwkv_kernel(*refs, has_vres, n_heads):
    if has_vres:
        (r_ref, k_ref, v_ref, l_ref, vf_ref, vec_ref, mul_ref, w2_ref, a2_ref, g2_ref, v2_ref,
         o_ref, s_sc, pr_sc, pk_sc, pv_sc, pl_sc, big_sc) = refs
        vout_ref = None
    else:
        (r_ref, k_ref, v_ref, l_ref, vec_ref, mul_ref, w2_ref, a2_ref, g2_ref,
         o_ref, vout_ref, s_sc, pr_sc, pk_sc, pv_sc, pl_sc, big_sc) = refs
    lc, width = r_ref.shape
    hd = RWKV_HEAD
    prec = HIGHEST

    @pl.when(pl.program_id(1) == 0)
    def _():
        s_sc[...] = jnp.zeros_like(s_sc)
        pr_sc[...] = jnp.zeros_like(pr_sc)
        pk_sc[...] = jnp.zeros_like(pk_sc)
        pv_sc[...] = jnp.zeros_like(pv_sc)
        pl_sc[...] = jnp.zeros_like(pl_sc)

    def vec(i):
        return vec_ref[i:i + 1, :]

    def shift_mix(x_ref, prev_sc, mu):
        x = x_ref[...]
        row = lax.broadcasted_iota(jnp.int32, x.shape, 0)
        prev = jnp.where(row == 0, prev_sc[...], pltpu.roll(x, 1, axis=0))
        prev_sc[...] = x[lc - 1:lc, :]
        return x + mu * (prev - x)

    xr = shift_mix(r_ref, pr_sc, vec(_V_MUR))
    xk = shift_mix(k_ref, pk_sc, vec(_V_MUK))
    xv = shift_mix(v_ref, pv_sc, vec(_V_MUV))
    xl = shift_mix(l_ref, pl_sc, mul_ref[...])
    xwa, xg, xvr = xl[:, 0:128], xl[:, 128:384], xl[:, 384:512]

    bdot = lambda a, b: jnp.dot(a.astype(BF16), b, preferred_element_type=F32)
    w_log = -_softplus(-(vec(_V_W0) + bdot(jnp.tanh(xwa), w2_ref[...]))) - 0.5
    logw = -jnp.exp(w_log)
    iclr = jax.nn.sigmoid(vec(_V_A0) + bdot(xwa, a2_ref[...]))
    gate = bdot(jax.nn.sigmoid(xg), g2_ref[...])
    if has_vres:
        v = xv + (vf_ref[...] - xv) * jax.nn.sigmoid(vec(_V_V0) + bdot(xvr, v2_ref[...]))
    else:
        v = xv
        vout_ref[...] = xv
    kk = xk * vec(_V_KK)
    k = xk * (1.0 + (iclr - 1.0) * vec(_V_KA))

    ti = lax.broadcasted_iota(jnp.int32, (lc, lc), 0)
    tj = lax.broadcasted_iota(jnp.int32, (lc, lc), 1)
    incl = ti >= tj
    strict = ti > tj
    cum = jnp.dot(incl.astype(F32), logw, preferred_element_type=F32, precision=HIGHEST)
    cum_last = cum[lc - 1:lc, :]
    dec_in = jnp.exp(cum)
    dec_ex = jnp.exp(cum - logw)
    dec_inv = jnp.exp(-cum)
    dec_tail = jnp.exp(cum_last - cum)

    big_sc[0] = xr * dec_in
    big_sc[1] = kk
    big_sc[2] = iclr
    big_sc[3] = k * dec_inv
    big_sc[4] = v
    big_sc[5] = dec_ex
    big_sc[6] = k * dec_tail
    big_sc[7] = xr * k * vec(_V_RK)
    big_sc[8] = dec_inv
    big_sc[9] = dec_tail
    big_sc[10] = jnp.broadcast_to(jnp.exp(cum_last), (lc, width))

    def pair(hp, carry):
        off = pl.multiple_of(hp * 2 * hd, 2 * hd)
        blk = lambda i: big_sc[i, :, pl.ds(off, 2 * hd)]
        rt2, kkraw2, iclr2, kt2, v2, dex2, kh2, rk2, dinv2, dtail2, wl2 = [blk(i) for i in range(11)]
        outs = []
        for e in range(2):
            sl = slice(e * hd, (e + 1) * hd)
            kkh = kkraw2[:, sl]
            kkh = kkh * lax.rsqrt(jnp.sum(kkh * kkh, axis=-1, keepdims=True) + 1e-12)
            a = -kkh
            b = kkh * iclr2[:, sl]
            at = a * dex2[:, sl]
            bt = b * dinv2[:, sl]
            bh = b * dtail2[:, sl]
            vh = v2[:, sl]
            s0 = s_sc[2 * hp + e]
            o, s_new = _rwkv_head(rt2[:, sl], at, bt, kt2[:, sl], vh, bh, kh2[:, sl],
                                  wl2[0:1, sl], s0, strict, incl, prec)
            s_sc[2 * hp + e] = s_new
            mean = jnp.mean(o, axis=-1, keepdims=True)
            cen = o - mean
            var = jnp.mean(cen * cen, axis=-1, keepdims=True)
            on = cen * lax.rsqrt(var + LNX_EPS)
            bonus = jnp.sum(rk2[:, sl], axis=-1, keepdims=True) * vh
            outs.append((on, bonus))
        on = jnp.concatenate([outs[0][0], outs[1][0]], axis=1)
        bonus = jnp.concatenate([outs[0][1], outs[1][1]], axis=1)
        big_sc[11, :, pl.ds(off, 2 * hd)] = on
        big_sc[12, :, pl.ds(off, 2 * hd)] = bonus
        return carry

    lax.fori_loop(0, n_heads // 2, pair, 0)
    o_ref[...] = (big_sc[11] * vec(_V_LNW) + vec(_V_LNB) + big_sc[12]) * gate


def _rwkv(p, vfirst, vecs, mu_l, w2, a2, g2p, v2p, *, batch, width, col0, lora_col):
    rows = p.shape[0]
    lc = RWKV_CHUNK
    tp = rows // batch
    nck = tp // lc
    n_heads = width // RWKV_HEAD
    has_vres = vfirst is not None
    rowblk = lambda b, c: b * nck + c
    cb = col0 // width
    tok = lambda j: pl.BlockSpec((lc, width), lambda b, c: (rowblk(b, c), j))
    full = lambda a: pl.BlockSpec(a.shape, lambda b, c: (0,) * a.ndim)
    in_specs = [tok(cb), tok(cb + 1), tok(cb + 2),
                pl.BlockSpec((lc, LORA_BLOCK), lambda b, c: (rowblk(b, c), lora_col // LORA_BLOCK))]
    args = [p, p, p, p]
    if has_vres:
        in_specs.append(tok(0))
        args.append(vfirst)
    in_specs += [full(vecs), full(mu_l), full(w2), full(a2), full(g2p)]
    args += [vecs, mu_l, w2, a2, g2p]
    if has_vres:
        in_specs.append(full(v2p))
        args.append(v2p)
    out_shape = [jax.ShapeDtypeStruct((rows, width), F32)]
    out_specs = [tok(0)]
    if not has_vres:
        out_shape.append(jax.ShapeDtypeStruct((rows, width), F32))
        out_specs.append(tok(0))
    outs = pl.pallas_call(
        functools.partial(_rwkv_kernel, has_vres=has_vres, n_heads=n_heads),
        grid=(batch, nck),
        in_specs=in_specs,
        out_specs=out_specs,
        out_shape=out_shape,
        scratch_shapes=[pltpu.VMEM((n_heads, RWKV_HEAD, RWKV_HEAD), F32),
                        pltpu.VMEM((1, width), F32), pltpu.VMEM((1, width), F32),
                        pltpu.VMEM((1, width), F32), pltpu.VMEM((1, LORA_BLOCK), F32),
                        pltpu.VMEM((13, lc, width), F32)],
        compiler_params=_cparams("parallel", "arbitrary"),
        name="rwkv7",
    )(*args)
    return outs if not has_vres else (outs[0], None)


def _merge_kernel(x_ref, ga_ref, gb_ref, u_ref, ys_ref, yb_ref, d_ref, wglu_ref, wua_ref, wub_ref,
                  wo_ref, o_ref):
    y = ys_ref[...] + d_ref[...] * u_ref[...]
    y = jax.nn.gelu(y, approximate=True)
    ya = y * jax.nn.sigmoid(jnp.dot(y.astype(BF16), wglu_ref[...], preferred_element_type=F32))
    up_a = jnp.dot(ya.astype(BF16), wua_ref[...], preferred_element_type=F32)
    up_b = jnp.dot(yb_ref[...].astype(BF16), wub_ref[...], preferred_element_type=F32)
    merged = jax.nn.sigmoid(ga_ref[...]) * up_a + jax.nn.sigmoid(gb_ref[...]) * up_b
    o_ref[...] = x_ref[...] + jnp.dot(merged.astype(BF16), wo_ref[...], preferred_element_type=F32)


def _merge(x, p, ys, yb, d_skip, wglu, wua, wub, wo, *, u_col):
    rows, d = x.shape
    w = ys.shape[1]
    tm = min(256, _row_tile(rows))
    row = lambda width, j: pl.BlockSpec((tm, width), lambda i: (i, j))
    const = lambda a: pl.BlockSpec(a.shape, lambda i: (0, 0), pipeline_mode=pl.Buffered(1))
    return pl.pallas_call(
        _merge_kernel,
        grid=(rows // tm,),
        in_specs=[row(d, 0), row(d, 0), row(d, 1), row(w, u_col // w), row(w, 0), row(w, 0),
                  const(d_skip), const(wglu), const(wua), const(wub), const(wo)],
        out_specs=row(d, 0),
        out_shape=jax.ShapeDtypeStruct((rows, d), F32),
        compiler_params=_cparams("parallel"),
        name="merge",
    )(x, p, p, p, ys, yb, d_skip, wglu, wua, wub, wo)


def _norm_kernel(x_ref, g_ref, o_ref):
    o_ref[...] = _rms(x_ref[...], g_ref[...])


def _final_norm(x, g):
    rows, d = x.shape
    tm = _row_tile(rows)
    return pl.pallas_call(
        _norm_kernel,
        grid=(rows // tm,),
        in_specs=[pl.BlockSpec((tm, d), lambda i: (i, 0)), pl.BlockSpec((1, d), lambda i: (0, 0))],
        out_specs=pl.BlockSpec((tm, d), lambda i: (i, 0)),
        out_shape=jax.ShapeDtypeStruct((rows, d), F32),
        compiler_params=_cparams("parallel"),
        name="final_norm",
    )(x, g)


def _pad_cols(a, width):
    return jnp.pad(a, ((0, 0), (0, width - a.shape[1])))


def _pad_rows(a, height):
    return jnp.pad(a, ((0, height - a.shape[0]), (0, 0)))


def kernel(x, meta_tokens, ffn1_norm, ffn1_w_gate, ffn1_w_up, ffn1_w_down, mix_norm, w_in_first, w_in_rest, mu_shift, mu_vres, ssm_lambda_re, ssm_lambda_im, ssm_log_dt, ssm_b_re, ssm_b_im, ssm_c_re, ssm_c_im, ssm_d, ssm_w_glu, rwkv_w0, rwkv_w2, rwkv_a0, rwkv_a2, rwkv_v0, rwkv_v2, rwkv_g2, rwkv_k_k, rwkv_k_a, rwkv_r_k, rwkv_lnx_w, rwkv_lnx_b, w_up_ssm, w_up_rwkv, w_out, ffn2_norm, ffn2_w_gate, ffn2_w_up, ffn2_w_down, final_norm):
    bsz, seq, d = x.shape
    n_meta = meta_tokens.shape[0]
    depth = ffn1_norm.shape[0]
    w = ssm_d.shape[1]
    n_groups = ssm_b_re.shape[1]
    ld, la, lg, lv = rwkv_w2.shape[1], rwkv_a2.shape[1], rwkv_g2.shape[1], rwkv_v2.shape[1]
    assert w == rwkv_w0.shape[1] and 2 * w == d and w % 128 == 0
    assert n_groups * SSM_GROUP == w and ld + la <= 128 and lg <= 256 and lv <= 128
    t_len = n_meta + seq
    tp = -(-t_len // SEQ_ALIGN) * SEQ_ALIGN
    rows = bsz * tp

    meta = jnp.broadcast_to(meta_tokens[None].astype(x.dtype), (bsz, n_meta, d))
    h_res = jnp.concatenate([meta, x, jnp.zeros((bsz, tp - t_len, d), x.dtype)], axis=1)
    h_res = h_res.reshape(rows, d)

    u_col = 2 * d
    rkv_col = u_col + w
    lora_col = rkv_col + 3 * w
    c_w = 4 * w
    c_g = c_w + ld + la
    c_ga = c_g + lg
    c_gb = c_ga + d
    p_common = c_gb + d

    v_first = None
    for i in range(depth):
        h_res = _ffn(h_res, ffn1_norm[i][None], ffn1_w_gate[i].astype(BF16),
                     ffn1_w_up[i].astype(BF16), ffn1_w_down[i].astype(BF16))

        w_in = w_in_first if i == 0 else w_in_rest[i - 1]
        blocks = [w_in[:, c_ga:c_gb], w_in[:, c_gb:p_common], w_in[:, :c_w],
                  _pad_cols(w_in[:, c_w:c_g], 128), _pad_cols(w_in[:, c_g:c_ga], 256),
                  _pad_cols(w_in[:, p_common:], 128)]
        p = _proj(h_res, mix_norm[i][None], jnp.concatenate(blocks, axis=1).astype(BF16))

        n_chunks = tp // S5_CHUNK
        u = p[:, u_col:u_col + w].reshape(bsz * n_chunks, S5_CHUNK, n_groups, SSM_GROUP)
        xt = jnp.transpose(u, (2, 0, 1, 3)).reshape(n_groups, bsz * n_chunks, S5_CHUNK * SSM_GROUP)
        yt = _s5_scan(xt, ssm_log_dt[i], ssm_lambda_re[i], ssm_lambda_im[i], ssm_c_re[i], ssm_c_im[i],
                      ssm_b_re[i], ssm_b_im[i], n_chunks)
        ys = jnp.transpose(yt.reshape(n_groups, bsz * n_chunks, S5_CHUNK, SSM_GROUP), (1, 2, 0, 3))
        ys = ys.reshape(rows, w)

        mu = mu_shift[i]
        zero_w = jnp.zeros((w,), F32)
        vecs = jnp.stack([rwkv_w0[i], rwkv_a0[i], rwkv_v0[i - 1] if i else zero_w, rwkv_k_k[i],
                          rwkv_k_a[i], rwkv_r_k[i].reshape(w), rwkv_lnx_w[i], rwkv_lnx_b[i],
                          mu[0:w], mu[w:2 * w], mu[2 * w:3 * w]])
        vecs = _pad_rows(vecs, _VEC_ROWS)
        mu_v = mu_vres[i - 1] if i else jnp.zeros((lv,), F32)
        mu_l = jnp.concatenate([jnp.pad(mu[3 * w:3 * w + ld + la], (0, 128 - ld - la)),
                                jnp.pad(mu[3 * w + ld + la:], (0, 256 - lg)),
                                jnp.pad(mu_v, (0, 128 - lv))])[None]
        w2p = _pad_rows(rwkv_w2[i], 128).astype(BF16)
        a2p = jnp.pad(rwkv_a2[i], ((ld, 128 - ld - la), (0, 0))).astype(BF16)
        g2p = _pad_rows(rwkv_g2[i], 256).astype(BF16)
        v2p = _pad_rows(rwkv_v2[i - 1], 128).astype(BF16) if i else None
        yb, v_new = _rwkv(p, v_first, vecs, mu_l, w2p, a2p, g2p, v2p,
                          batch=bsz, width=w, col0=rkv_col, lora_col=lora_col)
        if i == 0:
            v_first = v_new

        h_res = _merge(h_res, p, ys, yb, ssm_d[i][None], ssm_w_glu[i].astype(BF16),
                       w_up_ssm[i].astype(BF16), w_up_rwkv[i].astype(BF16), w_out[i].astype(BF16),
                       u_col=u_col)

        h_res = _ffn(h_res, ffn2_norm[i][None], ffn2_w_gate[i].astype(BF16),
                     ffn2_w_up[i].astype(BF16), ffn2_w_down[i].astype(BF16))

    out = _final_norm(h_res, final_norm[None]).reshape(bsz, tp, d)
    return out[:, n_meta:t_len]
```

```python
import functools
import math

import jax
import jax.numpy as jnp
from jax import lax
from jax.experimental import pallas as pl
from jax.experimental.pallas import tpu as pltpu

F32 = jnp.float32
BF16 = jnp.bfloat16
HIGHEST = lax.Precision.HIGHEST

NORM_EPS = 1e-6
RWKV_HEAD = 64
LNX_EPS = RWKV_HEAD * 1e-5
SSM_GROUP = 16
S5_CHUNK = 16
RWKV_CHUNK = 64
RWKV_PAIR_UNROLL = 8
LORA_BLOCK = 512
SEQ_ALIGN = 128
VMEM_LIMIT_BYTES = 56 * 1024 * 1024


def _cparams(*sem):
    return pltpu.CompilerParams(dimension_semantics=sem, vmem_limit_bytes=VMEM_LIMIT_BYTES)


def _rms(x, g):
    inv = lax.rsqrt(jnp.mean(x * x, axis=-1, keepdims=True) + NORM_EPS)
    return (x * inv) * g


def _row_tile(rows):
    for tm in (512, 256, 128, 64, 32, 16, 8):
        if rows % tm == 0:
            return tm
    raise ValueError(rows)


def _col_tile(cols):
    for tn in (512, 256, 128):
        if cols % tn == 0:
            return tn
    raise ValueError(cols)


def _ffn_kernel(x_ref, g_ref, wg_ref, wu_ref, wd_ref, o_ref, h_sc):
    f = pl.program_id(1)

    @pl.when(f == 0)
    def _():
        x = x_ref[...]
        h_sc[...] = _rms(x, g_ref[...]).astype(BF16)
        o_ref[...] = x

    h = h_sc[...]
    a = jnp.dot(h, wg_ref[...], preferred_element_type=F32)
    b = jnp.dot(h, wu_ref[...], preferred_element_type=F32)
    z = (0.5 * (a * jax.nn.sigmoid(a)) * b).astype(BF16)
    o_ref[...] += jnp.dot(z, wd_ref[...], preferred_element_type=F32)


def _ffn(x, g, wg, wu, wd):
    rows, d = x.shape
    dff = wg.shape[1]
    tm = _row_tile(rows)
    tf = _col_tile(dff)
    return pl.pallas_call(
        _ffn_kernel,
        grid=(rows // tm, dff // tf),
        in_specs=[
            pl.BlockSpec((tm, d), lambda i, f: (i, 0)),
            pl.BlockSpec((1, d), lambda i, f: (0, 0)),
            pl.BlockSpec((d, tf), lambda i, f: (0, f)),
            pl.BlockSpec((d, tf), lambda i, f: (0, f)),
            pl.BlockSpec((tf, d), lambda i, f: (f, 0)),
        ],
        out_specs=pl.BlockSpec((tm, d), lambda i, f: (i, 0)),
        out_shape=jax.ShapeDtypeStruct((rows, d), F32),
        scratch_shapes=[pltpu.VMEM((tm, d), BF16)],
        compiler_params=_cparams("parallel", "arbitrary"),
        name="ffn",
    )(x, g, wg, wu, wd)


def _proj_kernel(x_ref, g_ref, w_ref, o_ref, h_sc):
    @pl.when(pl.program_id(1) == 0)
    def _():
        h_sc[...] = _rms(x_ref[...], g_ref[...]).astype(BF16)

    o_ref[...] = jnp.dot(h_sc[...], w_ref[...], preferred_element_type=F32)


def _proj(x, g, w):
    rows, d = x.shape
    n = w.shape[1]
    tm = _row_tile(rows)
    tn = _col_tile(n)
    return pl.pallas_call(
        _proj_kernel,
        grid=(rows // tm, n // tn),
        in_specs=[
            pl.BlockSpec((tm, d), lambda i, j: (i, 0)),
            pl.BlockSpec((1, d), lambda i, j: (0, 0)),
            pl.BlockSpec((d, tn), lambda i, j: (0, j)),
        ],
        out_specs=pl.BlockSpec((tm, tn), lambda i, j: (i, j)),
        out_shape=jax.ShapeDtypeStruct((rows, n), F32),
        scratch_shapes=[pltpu.VMEM((tm, d), BF16)],
        compiler_params=_cparams("parallel", "arbitrary"),
        name="in_proj",
    )(x, g, w)


def _cpow(lr, li, dt, m):
    mag = jnp.exp(lr * dt * m)
    ang = li * dt * m
    return mag * jnp.cos(ang), mag * jnp.sin(ang)


def _s5_kernel(x_ref, ldt_ref, lrc_ref, lic_ref, lrr_ref, lir_ref, ctre_ref, ctim_ref,
               btre_ref, btim_ref, bttre_ref, bttim_ref, o_ref, m_sc, *, n_chunks):
    lc = S5_CHUNK
    cw = SSM_GROUP
    width = lc * cw
    rows = x_ref.shape[0]
    n_state = lrc_ref.shape[0]
    dt = jnp.exp(ldt_ref[...])
    lr_c, li_c = lrc_ref[...], lic_ref[...]
    lr_r, li_r = lrr_ref[...], lir_ref[...]

    are, aim = _cpow(lr_r, li_r, dt, 1.0)
    den = lr_r * lr_r + li_r * li_r
    nr, ni = are - 1.0, aim
    q_re = (nr * lr_r + ni * li_r) / den
    q_im = (ni * lr_r - nr * li_r) / den

    lag = (lax.broadcasted_iota(jnp.int32, (1, width), 1) // cw).astype(F32)
    ct_re, ct_im = ctre_ref[...], ctim_ref[...]
    p_re, p_im = _cpow(lr_c, li_c, dt, lag)
    d0_re = ct_re * p_re - ct_im * p_im
    d0_im = ct_re * p_im + ct_im * p_re
    bt_re, bt_im = btre_ref[...], btim_ref[...]
    bbt_re = q_re * bt_re - q_im * bt_im
    bbt_im = q_re * bt_im + q_im * bt_re
    m0 = (jnp.dot(bbt_re, d0_re, preferred_element_type=F32, precision=HIGHEST)
          - jnp.dot(bbt_im, d0_im, preferred_element_type=F32, precision=HIGHEST))
    lane = lax.broadcasted_iota(jnp.int32, (cw, width), 1)
    for j in range(lc):
        blk = m0 if j == 0 else jnp.where(lane >= j * cw, pltpu.roll(m0, j * cw, axis=1), 0.0)
        m_sc[j * cw:(j + 1) * cw, :] = blk

    rowj = (lax.broadcasted_iota(jnp.int32, (width, 1), 0) // cw).astype(F32)
    e_pre, e_pim = _cpow(lr_r, li_r, dt, (lc - 1.0) - rowj)
    btt_re, btt_im = bttre_ref[...], bttim_ref[...]
    bb2_re = q_re * btt_re - q_im * btt_im
    bb2_im = q_re * btt_im + q_im * btt_re
    e_mat = jnp.concatenate([bb2_re * e_pre - bb2_im * e_pim,
                             bb2_re * e_pim + bb2_im * e_pre], axis=1)

    p1_re, p1_im = _cpow(lr_c, li_c, dt, lag + 1.0)
    p_mat = jnp.concatenate([ct_re * p1_re - ct_im * p1_im,
                             -(ct_re * p1_im + ct_im * p1_re)], axis=0)

    x = x_ref[...].astype(BF16)
    y = jnp.dot(x, m_sc[...].astype(BF16), preferred_element_type=F32)
    s = jnp.dot(x, e_mat.astype(BF16), preferred_element_type=F32)

    crow = lax.broadcasted_iota(jnp.int32, (rows, 1), 0) % n_chunks
    shift = 1
    while shift < n_chunks:
        pre, pim = _cpow(lr_r, li_r, dt, float(lc * shift))
        a1 = jnp.concatenate([pre, pre], axis=1)
        a2 = jnp.concatenate([-pim, pim], axis=1)
        prev = jnp.where(crow >= shift, pltpu.roll(s, shift, axis=0), 0.0)
        s = s + a1 * prev + a2 * pltpu.roll(prev, n_state, axis=1)
        shift *= 2
    s_start = jnp.where(crow >= 1, pltpu.roll(s, 1, axis=0), 0.0)
    y = y + jnp.dot(s_start.astype(BF16), p_mat.astype(BF16), preferred_element_type=F32)
    o_ref[...] = y


def _s5_scan(xt, ldt, lr, li, c_re, c_im, b_re, b_im, n_chunks):
    g, rows, width = xt.shape
    n = lr.shape[1]
    cw = SSM_GROUP
    lc = S5_CHUNK
    ct_re = jnp.tile(jnp.swapaxes(c_re, 1, 2), (1, 1, lc))
    ct_im = jnp.tile(jnp.swapaxes(c_im, 1, 2), (1, 1, lc))
    bt_re = jnp.swapaxes(b_re, 1, 2)
    bt_im = jnp.swapaxes(b_im, 1, 2)
    btt_re = jnp.tile(bt_re, (1, lc, 1))
    btt_im = jnp.tile(bt_im, (1, lc, 1))
    g3 = lambda a, b: pl.BlockSpec((None, a, b), lambda i: (i, 0, 0))
    return pl.pallas_call(
        functools.partial(_s5_kernel, n_chunks=n_chunks),
        grid=(g,),
        in_specs=[g3(rows, width), g3(1, 1), g3(n, 1), g3(n, 1), g3(1, n), g3(1, n),
                  g3(n, width), g3(n, width), g3(cw, n), g3(cw, n), g3(width, n), g3(width, n)],
        out_specs=g3(rows, width),
        out_shape=jax.ShapeDtypeStruct((g, rows, width), F32),
        scratch_shapes=[pltpu.VMEM((width, width), F32)],
        compiler_params=_cparams("parallel"),
        name="s5_scan",
    )(xt, ldt.reshape(g, 1, 1), lr.reshape(g, n, 1), li.reshape(g, n, 1), lr.reshape(g, 1, n),
      li.reshape(g, 1, n), ct_re, ct_im, bt_re, bt_im, btt_re, btt_im)


(_V_W0, _V_A0, _V_V0, _V_KK, _V_KA, _V_RK, _V_LNW, _V_LNB, _V_MUR, _V_MUK, _V_MUV) = range(11)
_VEC_ROWS = 16


def _softplus(z):
    return jnp.maximum(z, 0.0) + jnp.log(1.0 + jnp.exp(-jnp.abs(z)))


def _dot_nt(a, b, **kw):
    return lax.dot_general(a, b, (((1,), (1,)), ((), ())), preferred_element_type=F32, **kw)


def _dot_tn(a, b, **kw):
    return lax.dot_general(a, b, (((0,), (0,)), ((), ())), preferred_element_type=F32, **kw)


def _mm(a, b):
    return jnp.dot(a.astype(BF16), b.astype(BF16), preferred_element_type=F32)


def _rwkv_pairs(pairs):
    lc, pw = pairs[0][0].shape
    hd = RWKV_HEAD
    assert lc == hd and pw == 2 * hd
    lane = lax.broadcasted_iota(jnp.int32, (lc, pw), 1)
    row = lax.broadcasted_iota(jnp.int32, (lc, pw), 0)
    first = lane < hd
    col = jnp.where(first, lane, lane - hd)
    strict2 = row > col
    incl2 = row >= col
    ak_mask = jnp.logical_and(strict2, jnp.logical_not(first))
    zeros = jnp.zeros((lc, pw), F32)
    heads = [(p, e) for p in range(len(pairs)) for e in range(2)]

    gms = []
    for rt, at, bt, kt, v, bh, kh, wl, s0 in pairs:
        ar = jnp.concatenate([jnp.where(first, at, 0.0), jnp.where(first, rt, 0.0),
                              jnp.where(first, 0.0, at), jnp.where(first, 0.0, rt)], axis=0)
        gms.append(_dot_nt(ar.astype(BF16), jnp.concatenate([bt, kt], axis=0).astype(BF16)))
    ga = [gms[p][2 * e * lc:(2 * e + 1) * lc] for p, e in heads]
    gr = [gms[p][(2 * e + 1) * lc:(2 * e + 2) * lc] for p, e in heads]
    vz = [jnp.concatenate([zeros, pr[4]], axis=0).astype(BF16) for pr in pairs]
    zv = [jnp.concatenate([zeros, pr[4]], axis=1) for pr in pairs]

    nk = [jnp.where(strict2[:, :lc], g[:, :lc], 0.0).astype(BF16) for g in ga]
    y = [jnp.concatenate([pairs[p][1], _mm(jnp.where(ak_mask, ga[i], 0.0), vz[p])], axis=1)
         for i, (p, e) in enumerate(heads)]
    steps = max(1, (lc - 1).bit_length())
    for i in range(steps):
        y = [yy + _mm(nn, yy) for nn, yy in zip(nk, y)]
        if i + 1 < steps:
            nk = [_mm(nn, nn).astype(BF16) for nn in nk]
    rz = [_mm(jnp.where(incl2, gr[i], 0.0), jnp.concatenate([y[i], zv[p]], axis=0))
          for i, (p, e) in enumerate(heads)]

    x2s, z2s, uos = [], [], []
    for p, (rt, at, bt, kt, v, bh, kh, wl, s0) in enumerate(pairs):
        y0, y1, rz0, rz1 = y[2 * p], y[2 * p + 1], rz[2 * p], rz[2 * p + 1]
        x1 = jnp.where(first, y0[:, :pw], y1[:, :pw])
        z1 = rt + jnp.where(first, rz0[:, :pw], rz1[:, :pw])
        x2s.append(jnp.where(first, y0[:, pw:], y1[:, pw:]))
        z2s.append(jnp.where(first, rz0[:, pw:], rz1[:, pw:]))
        uos.append(_dot_nt(jnp.concatenate([x1, z1], axis=0).astype(BF16), s0.astype(BF16)))
    si = lax.broadcasted_iota(jnp.int32, (pw, pw), 0)
    sj = lax.broadcasted_iota(jnp.int32, (pw, pw), 1)
    same_head = (si < hd) == (sj < hd)
    outs = []
    for p, (rt, at, bt, kt, v, bh, kh, wl, s0) in enumerate(pairs):
        u = uos[p][:lc] + x2s[p]
        o = uos[p][lc:] + z2s[p]
        upd = _dot_tn(jnp.concatenate([u, v], axis=0).astype(BF16),
                      jnp.concatenate([bh, kh], axis=0).astype(BF16))
        outs.append((o, s0 * wl + jnp.where(same_head, upd, 0.0)))
    return outs


def _rwkv_kernel(*refs, has_vres, n_heads):
    if has_vres:
        (r_ref, k_ref, v_ref, l_ref, vf_ref, vec_ref, mul_ref, w2_ref, a2_ref, g2_ref, v2_ref,
         o_ref, s_sc, pr_sc, pk_sc, pv_sc, pl_sc, big_sc, on_sc, bonus_sc) = refs
        vout_ref = None
    else:
        (r_ref, k_ref, v_ref, l_ref, vec_ref, mul_ref, w2_ref, a2_ref, g2_ref,
         o_ref, vout_ref, s_sc, pr_sc, pk_sc, pv_sc, pl_sc, big_sc, on_sc, bonus_sc) = refs
    lc, width = r_ref.shape
    hd = RWKV_HEAD
    pw = 2 * hd

    @pl.when(pl.program_id(1) == 0)
    def _():
        s_sc[...] = jnp.zeros_like(s_sc)
        pr_sc[...] = jnp.zeros_like(pr_sc)
        pk_sc[...] = jnp.zeros_like(pk_sc)
        pv_sc[...] = jnp.zeros_like(pv_sc)
        pl_sc[...] = jnp.zeros_like(pl_sc)

    def vec(i):
        return vec_ref[i:i + 1, :]

    def shift_mix(x_ref, prev_sc, mu):
        x = x_ref[...]
        row = lax.broadcasted_iota(jnp.int32, x.shape, 0)
        prev = jnp.where(row == 0, prev_sc[...], pltpu.roll(x, 1, axis=0))
        prev_sc[...] = x[lc - 1:lc, :]
        return x + mu * (prev - x)

    xr = shift_mix(r_ref, pr_sc, vec(_V_MUR))
    xk = shift_mix(k_ref, pk_sc, vec(_V_MUK))
    xv = shift_mix(v_ref, pv_sc, vec(_V_MUV))
    xl = shift_mix(l_ref, pl_sc, mul_ref[...])
    xwa, xg, xvr = xl[:, 0:128], xl[:, 128:384], xl[:, 384:512]

    w_log = -_softplus(-(vec(_V_W0) + _mm(jnp.tanh(xwa), w2_ref[...]))) - 0.5
    logw = -jnp.exp(w_log)
    iclr = jax.nn.sigmoid(vec(_V_A0) + _mm(xwa, a2_ref[...]))
    gate = _mm(jax.nn.sigmoid(xg), g2_ref[...])
    if has_vres:
        v = xv + (vf_ref[...] - xv) * jax.nn.sigmoid(vec(_V_V0) + _mm(xvr, v2_ref[...]))
    else:
        v = xv
        vout_ref[...] = xv
    k = xk * (1.0 + (iclr - 1.0) * vec(_V_KA))

    ti = lax.broadcasted_iota(jnp.int32, (lc, lc), 0)
    tj = lax.broadcasted_iota(jnp.int32, (lc, lc), 1)
    cum = jnp.dot((ti >= tj).astype(F32), logw, preferred_element_type=F32, precision=HIGHEST)
    cum_last = cum[lc - 1:lc, :]
    dec_inv = jnp.exp(-cum)
    dec_tail = jnp.exp(cum_last - cum)

    big_sc[0] = xr * jnp.exp(cum)
    big_sc[1] = xk * vec(_V_KK)
    big_sc[2] = iclr
    big_sc[3] = k * dec_inv
    big_sc[4] = v
    big_sc[5] = jnp.exp(cum - logw)
    big_sc[6] = k * dec_tail
    big_sc[7] = xr * k * vec(_V_RK)
    big_sc[8] = dec_inv
    big_sc[9] = dec_tail
    big_sc[10] = jnp.broadcast_to(jnp.exp(cum_last), (lc, width))

    first = lax.broadcasted_iota(jnp.int32, (lc, pw), 1) < hd

    def head_sum(x):
        s0 = jnp.sum(jnp.where(first, x, 0.0), axis=-1, keepdims=True)
        s1 = jnp.sum(jnp.where(first, 0.0, x), axis=-1, keepdims=True)
        return jnp.where(first, s0, s1)

    n_pairs = n_heads // 2
    per_iter = min(RWKV_PAIR_UNROLL, n_pairs)

    def pair_group(g, carry):
        ids, ins, extra = [], [], []
        for j in range(per_iter):
            hp = g * per_iter + j
            off = hp * pw if isinstance(hp, int) else pl.multiple_of(hp * pw, pw)
            rt, kkraw, iclr_p, kt, v_p, dex, kh, rk, dinv, dtail, wl = [
                big_sc[i, :, pl.ds(off, pw)] for i in range(11)]
            kkn = kkraw * lax.rsqrt(head_sum(kkraw * kkraw) + 1e-12)
            b = kkn * iclr_p
            ids.append((hp, off))
            ins.append((rt, -kkn * dex, b * dinv, kt, v_p, b * dtail, kh, wl[0:1], s_sc[hp]))
            extra.append(head_sum(rk) * v_p)
        outs = _rwkv_pairs(ins)
        for (hp, off), (o, s_new), bonus in zip(ids, outs, extra):
            cen = o - head_sum(o) * (1.0 / hd)
            var = head_sum(cen * cen) * (1.0 / hd)
            s_sc[hp] = s_new
            on_sc[:, pl.ds(off, pw)] = cen * lax.rsqrt(var + LNX_EPS)
            bonus_sc[:, pl.ds(off, pw)] = bonus
        return carry

    if per_iter == n_pairs:
        pair_group(0, 0)
    else:
        lax.fori_loop(0, n_pairs // per_iter, pair_group, 0)
    o_ref[...] = (on_sc[...] * vec(_V_LNW) + vec(_V_LNB) + bonus_sc[...]) * gate


def _rwkv(p, vfirst, vecs, mu_l, w2, a2, g2p, v2p, *, batch, width, col0, lora_col):
    rows = p.shape[0]
    lc = RWKV_CHUNK
    tp = rows // batch
    nck = tp // lc
    n_heads = width // RWKV_HEAD
    has_vres = vfirst is not None
    rowblk = lambda b, c: b * nck + c
    cb = col0 // width
    tok = lambda j: pl.BlockSpec((lc, width), lambda b, c: (rowblk(b, c), j))
    full = lambda a: pl.BlockSpec(a.shape, lambda b, c: (0,) * a.ndim)
    in_specs = [tok(cb), tok(cb + 1), tok(cb + 2),
                pl.BlockSpec((lc, LORA_BLOCK), lambda b, c: (rowblk(b, c), lora_col // LORA_BLOCK))]
    args = [p, p, p, p]
    if has_vres:
        in_specs.append(tok(0))
        args.append(vfirst)
    in_specs += [full(vecs), full(mu_l), full(w2), full(a2), full(g2p)]
    args += [vecs, mu_l, w2, a2, g2p]
    if has_vres:
        in_specs.append(full(v2p))
        args.append(v2p)
    out_shape = [jax.ShapeDtypeStruct((rows, width), F32)]
    out_specs = [tok(0)]
    if not has_vres:
        out_shape.append(jax.ShapeDtypeStruct((rows, width), F32))
        out_specs.append(tok(0))
    outs = pl.pallas_call(
        functools.partial(_rwkv_kernel, has_vres=has_vres, n_heads=n_heads),
        grid=(batch, nck),
        in_specs=in_specs,
        out_specs=out_specs,
        out_shape=out_shape,
        scratch_shapes=[pltpu.VMEM((n_heads // 2, 2 * RWKV_HEAD, 2 * RWKV_HEAD), F32),
                        pltpu.VMEM((1, width), F32), pltpu.VMEM((1, width), F32),
                        pltpu.VMEM((1, width), F32), pltpu.VMEM((1, LORA_BLOCK), F32),
                        pltpu.VMEM((11, lc, width), F32),
                        pltpu.VMEM((lc, width), F32), pltpu.VMEM((lc, width), F32)],
        compiler_params=_cparams("parallel", "arbitrary"),
        name="rwkv7",
    )(*args)
    return outs if not has_vres else (outs[0], None)


def _merge_kernel(x_ref, ga_ref, gb_ref, u_ref, ys_ref, yb_ref, d_ref, wglu_ref, wua_ref, wub_ref,
                  wo_ref, o_ref):
    y = ys_ref[...] + d_ref[...] * u_ref[...]
    y = jax.nn.gelu(y, approximate=True)
    ya = y * jax.nn.sigmoid(jnp.dot(y.astype(BF16), wglu_ref[...], preferred_element_type=F32))
    up_a = jnp.dot(ya.astype(BF16), wua_ref[...], preferred_element_type=F32)
    up_b = jnp.dot(yb_ref[...].astype(BF16), wub_ref[...], preferred_element_type=F32)
    merged = jax.nn.sigmoid(ga_ref[...]) * up_a + jax.nn.sigmoid(gb_ref[...]) * up_b
    o_ref[...] = x_ref[...] + jnp.dot(merged.astype(BF16), wo_ref[...], preferred_element_type=F32)


def _merge(x, p, ys, yb, d_skip, wglu, wua, wub, wo, *, u_col):
    rows, d = x.shape
    w = ys.shape[1]
    tm = min(256, _row_tile(rows))
    row = lambda width, j: pl.BlockSpec((tm, width), lambda i: (i, j))
    const = lambda a: pl.BlockSpec(a.shape, lambda i: (0, 0), pipeline_mode=pl.Buffered(1))
    return pl.pallas_call(
        _merge_kernel,
        grid=(rows // tm,),
        in_specs=[row(d, 0), row(d, 0), row(d, 1), row(w, u_col // w), row(w, 0), row(w, 0),
                  const(d_skip), const(wglu), const(wua), const(wub), const(wo)],
        out_specs=row(d, 0),
        out_shape=jax.ShapeDtypeStruct((rows, d), F32),
        compiler_params=_cparams("parallel"),
        name="merge",
    )(x, p, p, p, ys, yb, d_skip, wglu, wua, wub, wo)


def _norm_kernel(x_ref, g_ref, o_ref):
    o_ref[...] = _rms(x_ref[...], g_ref[...])


def _final_norm(x, g):
    rows, d = x.shape
    tm = _row_tile(rows)
    return pl.pallas_call(
        _norm_kernel,
        grid=(rows // tm,),
        in_specs=[pl.BlockSpec((tm, d), lambda i: (i, 0)), pl.BlockSpec((1, d), lambda i: (0, 0))],
        out_specs=pl.BlockSpec((tm, d), lambda i: (i, 0)),
        out_shape=jax.ShapeDtypeStruct((rows, d), F32),
        compiler_params=_cparams("parallel"),
        name="final_norm",
    )(x, g)


def _pad_cols(a, width):
    return jnp.pad(a, ((0, 0), (0, width - a.shape[1])))


def _pad_rows(a, height):
    return jnp.pad(a, ((0, height - a.shape[0]), (0, 0)))


def kernel(x, meta_tokens, ffn1_norm, ffn1_w_gate, ffn1_w_up, ffn1_w_down, mix_norm, w_in_first, w_in_rest, mu_shift, mu_vres, ssm_lambda_re, ssm_lambda_im, ssm_log_dt, ssm_b_re, ssm_b_im, ssm_c_re, ssm_c_im, ssm_d, ssm_w_glu, rwkv_w0, rwkv_w2, rwkv_a0, rwkv_a2, rwkv_v0, rwkv_v2, rwkv_g2, rwkv_k_k, rwkv_k_a, rwkv_r_k, rwkv_lnx_w, rwkv_lnx_b, w_up_ssm, w_up_rwkv, w_out, ffn2_norm, ffn2_w_gate, ffn2_w_up, ffn2_w_down, final_norm):
    bsz, seq, d = x.shape
    n_meta = meta_tokens.shape[0]
    depth = ffn1_norm.shape[0]
    w = ssm_d.shape[1]
    n_groups = ssm_b_re.shape[1]
    ld, la, lg, lv = rwkv_w2.shape[1], rwkv_a2.shape[1], rwkv_g2.shape[1], rwkv_v2.shape[1]
    assert w == rwkv_w0.shape[1] and 2 * w == d and w % 128 == 0
    assert n_groups * SSM_GROUP == w and ld + la <= 128 and lg <= 256 and lv <= 128
    t_len = n_meta + seq
    tp = -(-t_len // SEQ_ALIGN) * SEQ_ALIGN
    rows = bsz * tp

    meta = jnp.broadcast_to(meta_tokens[None].astype(x.dtype), (bsz, n_meta, d))
    h_res = jnp.concatenate([meta, x, jnp.zeros((bsz, tp - t_len, d), x.dtype)], axis=1)
    h_res = h_res.reshape(rows, d)

    u_col = 2 * d
    rkv_col = u_col + w
    lora_col = rkv_col + 3 * w
    c_w = 4 * w
    c_g = c_w + ld + la
    c_ga = c_g + lg
    c_gb = c_ga + d
    p_common = c_gb + d

    v_first = None
    for i in range(depth):
        h_res = _ffn(h_res, ffn1_norm[i][None], ffn1_w_gate[i].astype(BF16),
                     ffn1_w_up[i].astype(BF16), ffn1_w_down[i].astype(BF16))

        w_in = w_in_first if i == 0 else w_in_rest[i - 1]
        blocks = [w_in[:, c_ga:c_gb], w_in[:, c_gb:p_common], w_in[:, :c_w],
                  _pad_cols(w_in[:, c_w:c_g], 128), _pad_cols(w_in[:, c_g:c_ga], 256),
                  _pad_cols(w_in[:, p_common:], 128)]
        p = _proj(h_res, mix_norm[i][None], jnp.concatenate(blocks, axis=1).astype(BF16))

        n_chunks = tp // S5_CHUNK
        u = p[:, u_col:u_col + w].reshape(bsz * n_chunks, S5_CHUNK, n_groups, SSM_GROUP)
        xt = jnp.transpose(u, (2, 0, 1, 3)).reshape(n_groups, bsz * n_chunks, S5_CHUNK * SSM_GROUP)
        yt = _s5_scan(xt, ssm_log_dt[i], ssm_lambda_re[i], ssm_lambda_im[i], ssm_c_re[i], ssm_c_im[i],
                      ssm_b_re[i], ssm_b_im[i], n_chunks)
        ys = jnp.transpose(yt.reshape(n_groups, bsz * n_chunks, S5_CHUNK, SSM_GROUP), (1, 2, 0, 3))
        ys = ys.reshape(rows, w)

        mu = mu_shift[i]
        zero_w = jnp.zeros((w,), F32)
        vecs = jnp.stack([rwkv_w0[i], rwkv_a0[i], rwkv_v0[i - 1] if i else zero_w, rwkv_k_k[i],
                          rwkv_k_a[i], rwkv_r_k[i].reshape(w), rwkv_lnx_w[i], rwkv_lnx_b[i],
                          mu[0:w], mu[w:2 * w], mu[2 * w:3 * w]])
        vecs = _pad_rows(vecs, _VEC_ROWS)
        mu_v = mu_vres[i - 1] if i else jnp.zeros((lv,), F32)
        mu_l = jnp.concatenate([jnp.pad(mu[3 * w:3 * w + ld + la], (0, 128 - ld - la)),
                                jnp.pad(mu[3 * w + ld + la:], (0, 256 - lg)),
                                jnp.pad(mu_v, (0, 128 - lv))])[None]
        w2p = _pad_rows(rwkv_w2[i], 128).astype(BF16)
        a2p = jnp.pad(rwkv_a2[i], ((ld, 128 - ld - la), (0, 0))).astype(BF16)
        g2p = _pad_rows(rwkv_g2[i], 256).astype(BF16)
        v2p = _pad_rows(rwkv_v2[i - 1], 128).astype(BF16) if i else None
        yb, v_new = _rwkv(p, v_first, vecs, mu_l, w2p, a2p, g2p, v2p,
                          batch=bsz, width=w, col0=rkv_col, lora_col=lora_col)
        if i == 0:
            v_first = v_new

        h_res = _merge(h_res, p, ys, yb, ssm_d[i][None], ssm_w_glu[i].astype(BF16),
                       w_up_ssm[i].astype(BF16), w_up_rwkv[i].astype(BF16), w_out[i].astype(BF16),
                       u_col=u_col)

        h_res = _ffn(h_res, ffn2_norm[i][None], ffn2_w_gate[i].astype(BF16),
                     ffn2_w_up[i].astype(BF16), ffn2_w_down[i].astype(BF16))

    out = _final_norm(h_res, final_norm[None]).reshape(bsz, tp, d)
    return out[:, n_meta:t_len]
```

```python
import functools
import math

import jax
import jax.numpy as jnp
from jax import lax
from jax.experimental import pallas as pl
from jax.experimental.pallas import tpu as pltpu

F32 = jnp.float32
BF16 = jnp.bfloat16
HIGHEST = lax.Precision.HIGHEST

NORM_EPS = 1e-6
RWKV_HEAD = 64
LNX_EPS = RWKV_HEAD * 1e-5
SSM_GROUP = 16
S5_CHUNK = 16
RWKV_CHUNK = 64
RWKV_PAIR_UNROLL = 8
PROJ_ROW_TILE = 1536
LORA_BLOCK = 512
SEQ_ALIGN = 128
VMEM_LIMIT_BYTES = 56 * 1024 * 1024


def _cparams(*sem):
    return pltpu.CompilerParams(dimension_semantics=sem, vmem_limit_bytes=VMEM_LIMIT_BYTES)


def _rms(x, g):
    inv = lax.rsqrt(jnp.mean(x * x, axis=-1, keepdims=True) + NORM_EPS)
    return (x * inv) * g


def _row_tile(rows):
    for tm in (512, 256, 128, 64, 32, 16, 8):
        if rows % tm == 0:
            return tm
    raise ValueError(rows)


def _col_tile(cols):
    for tn in (512, 256, 128):
        if cols % tn == 0:
            return tn
    raise ValueError(cols)


def _ffn_kernel(x_ref, g_ref, wg_ref, wu_ref, wd_ref, o_ref, h_sc):
    f = pl.program_id(1)

    @pl.when(f == 0)
    def _():
        x = x_ref[...]
        h_sc[...] = _rms(x, g_ref[...]).astype(BF16)
        o_ref[...] = x

    h = h_sc[...]
    a = jnp.dot(h, wg_ref[...], preferred_element_type=F32)
    b = jnp.dot(h, wu_ref[...], preferred_element_type=F32)
    z = (0.5 * (a * jax.nn.sigmoid(a)) * b).astype(BF16)
    o_ref[...] += jnp.dot(z, wd_ref[...], preferred_element_type=F32)


def _ffn(x, g, wg, wu, wd):
    rows, d = x.shape
    dff = wg.shape[1]
    tm = _row_tile(rows)
    tf = _col_tile(dff)
    return pl.pallas_call(
        _ffn_kernel,
        grid=(rows // tm, dff // tf),
        in_specs=[
            pl.BlockSpec((tm, d), lambda i, f: (i, 0)),
            pl.BlockSpec((1, d), lambda i, f: (0, 0)),
            pl.BlockSpec((d, tf), lambda i, f: (0, f)),
            pl.BlockSpec((d, tf), lambda i, f: (0, f)),
            pl.BlockSpec((tf, d), lambda i, f: (f, 0)),
        ],
        out_specs=pl.BlockSpec((tm, d), lambda i, f: (i, 0)),
        out_shape=jax.ShapeDtypeStruct((rows, d), F32),
        scratch_shapes=[pltpu.VMEM((tm, d), BF16)],
        compiler_params=_cparams("parallel", "arbitrary"),
        name="ffn",
    )(x, g, wg, wu, wd)


def _proj_kernel(x_ref, g_ref, w_ref, o_ref, h_sc):
    @pl.when(pl.program_id(1) == 0)
    def _():
        h_sc[...] = _rms(x_ref[...], g_ref[...]).astype(BF16)

    o_ref[...] = jnp.dot(h_sc[...], w_ref[...], preferred_element_type=F32)


def _proj(x, g, w):
    rows, d = x.shape
    n = w.shape[1]
    tm = PROJ_ROW_TILE if rows % PROJ_ROW_TILE == 0 else _row_tile(rows)
    tn = _col_tile(n)
    return pl.pallas_call(
        _proj_kernel,
        grid=(rows // tm, n // tn),
        in_specs=[
            pl.BlockSpec((tm, d), lambda i, j: (i, 0)),
            pl.BlockSpec((1, d), lambda i, j: (0, 0)),
            pl.BlockSpec((d, tn), lambda i, j: (0, j)),
        ],
        out_specs=pl.BlockSpec((tm, tn), lambda i, j: (i, j)),
        out_shape=jax.ShapeDtypeStruct((rows, n), F32),
        scratch_shapes=[pltpu.VMEM((tm, d), BF16)],
        compiler_params=_cparams("parallel", "arbitrary"),
        name="in_proj",
    )(x, g, w)


def _cpow(lr, li, dt, m):
    mag = jnp.exp(lr * dt * m)
    ang = li * dt * m
    return mag * jnp.cos(ang), mag * jnp.sin(ang)


def _s5_param_kernel(ldt_ref, lrc_ref, lic_ref, lrr_ref, lir_ref, ctre_ref, ctim_ref,
                     btre_ref, btim_ref, bttre_ref, bttim_ref, m0_ref, e_ref, p_ref, ap_ref,
                     *, n_chunks):
    lc = S5_CHUNK
    cw = SSM_GROUP
    width = lc * cw
    dt = jnp.exp(ldt_ref[...])
    lr_c, li_c = lrc_ref[...], lic_ref[...]
    lr_r, li_r = lrr_ref[...], lir_ref[...]

    are, aim = _cpow(lr_r, li_r, dt, 1.0)
    den = lr_r * lr_r + li_r * li_r
    nr, ni = are - 1.0, aim
    q_re = (nr * lr_r + ni * li_r) / den
    q_im = (ni * lr_r - nr * li_r) / den

    lag = (lax.broadcasted_iota(jnp.int32, (1, width), 1) // cw).astype(F32)
    ct_re, ct_im = ctre_ref[...], ctim_ref[...]
    p_re, p_im = _cpow(lr_c, li_c, dt, lag)
    d0_re = ct_re * p_re - ct_im * p_im
    d0_im = ct_re * p_im + ct_im * p_re
    bt_re, bt_im = btre_ref[...], btim_ref[...]
    bbt_re = q_re * bt_re - q_im * bt_im
    bbt_im = q_re * bt_im + q_im * bt_re
    m0_ref[...] = (jnp.dot(bbt_re, d0_re, preferred_element_type=F32, precision=HIGHEST)
                   - jnp.dot(bbt_im, d0_im, preferred_element_type=F32, precision=HIGHEST))

    rowj = (lax.broadcasted_iota(jnp.int32, (width, 1), 0) // cw).astype(F32)
    e_pre, e_pim = _cpow(lr_r, li_r, dt, (lc - 1.0) - rowj)
    btt_re, btt_im = bttre_ref[...], bttim_ref[...]
    bb2_re = q_re * btt_re - q_im * btt_im
    bb2_im = q_re * btt_im + q_im * btt_re
    e_ref[...] = jnp.concatenate([bb2_re * e_pre - bb2_im * e_pim,
                                  bb2_re * e_pim + bb2_im * e_pre], axis=1)

    p1_re, p1_im = _cpow(lr_c, li_c, dt, lag + 1.0)
    p_ref[...] = jnp.concatenate([ct_re * p1_re - ct_im * p1_im,
                                  -(ct_re * p1_im + ct_im * p1_re)], axis=0)

    ap_ref[...] = jnp.zeros_like(ap_ref)
    for i, shift in enumerate(_scan_shifts(n_chunks)):
        pre, pim = _cpow(lr_r, li_r, dt, float(lc * shift))
        ap_ref[2 * i:2 * i + 1, :] = jnp.concatenate([pre, pre], axis=1)
        ap_ref[2 * i + 1:2 * i + 2, :] = jnp.concatenate([-pim, pim], axis=1)


def _scan_shifts(n_chunks):
    shifts, s = [], 1
    while s < n_chunks:
        shifts.append(s)
        s *= 2
    return shifts


def _s5_main_kernel(*refs, n_chunks, n_state):
    lc = S5_CHUNK
    x_refs = refs[:lc]
    m0_ref, e_ref, p_ref, ap_ref, y_ref, m_sc = refs[lc:]
    lb = x_refs[0].shape[1]

    @pl.when(pl.program_id(1) == 0)
    def _():
        m0 = m0_ref[...]
        for t in range(lc):
            blk = m0 if t == 0 else jnp.concatenate(
                [jnp.zeros((lb, t * lb), m0.dtype), m0[:, :(lc - t) * lb]], axis=1)
            m_sc[t * lb:(t + 1) * lb, :] = blk

    x = jnp.concatenate([r[...].astype(BF16) for r in x_refs], axis=1)
    y = jnp.dot(x, m_sc[...], preferred_element_type=F32)
    s = jnp.dot(x, e_ref[...], preferred_element_type=F32)

    def swap_re_im(a):
        w2 = 2 * n_state
        return jnp.concatenate([pltpu.roll(a[:, j * w2:(j + 1) * w2], n_state, axis=1)
                                for j in range(a.shape[1] // w2)], axis=1)

    crow = lax.broadcasted_iota(jnp.int32, (n_chunks, 1), 0)
    for i, shift in enumerate(_scan_shifts(n_chunks)):
        prev = jnp.where(crow >= shift, pltpu.roll(s, shift, axis=0), 0.0)
        s = s + ap_ref[2 * i:2 * i + 1, :] * prev + ap_ref[2 * i + 1:2 * i + 2, :] * swap_re_im(prev)
    s_start = jnp.where(crow >= 1, pltpu.roll(s, 1, axis=0), 0.0)
    y = y + jnp.dot(s_start.astype(BF16), p_ref[...], preferred_element_type=F32)
    for t in range(lc):
        y_ref[t] = y[:, t * lb:(t + 1) * lb]


def _s5_scan(u, ldt, lr, li, c_re, c_im, b_re, b_im, *, batch):
    rows, w = u.shape
    g, n = lr.shape
    cw = SSM_GROUP
    lc = S5_CHUNK
    width = lc * cw
    lb = 128
    gpb = lb // cw
    n_oct = w // lb
    n_chunks = rows // batch // lc
    ap_rows = -(-2 * len(_scan_shifts(n_chunks)) // 8) * 8
    ct_re = jnp.tile(jnp.swapaxes(c_re, 1, 2), (1, 1, lc))
    ct_im = jnp.tile(jnp.swapaxes(c_im, 1, 2), (1, 1, lc))
    bt_re = jnp.swapaxes(b_re, 1, 2)
    bt_im = jnp.swapaxes(b_im, 1, 2)
    btt_re = jnp.tile(bt_re, (1, lc, 1))
    btt_im = jnp.tile(bt_im, (1, lc, 1))
    g3 = lambda a, b: pl.BlockSpec((None, a, b), lambda i: (i, 0, 0))
    m0, e_g, p_g, ap = pl.pallas_call(
        functools.partial(_s5_param_kernel, n_chunks=n_chunks),
        grid=(g,),
        in_specs=[g3(1, 1), g3(n, 1), g3(n, 1), g3(1, n), g3(1, n),
                  g3(n, width), g3(n, width), g3(cw, n), g3(cw, n), g3(width, n), g3(width, n)],
        out_specs=[g3(cw, width), g3(width, 2 * n), g3(2 * n, width), g3(ap_rows, 2 * n)],
        out_shape=[jax.ShapeDtypeStruct((g, cw, width), F32),
                   jax.ShapeDtypeStruct((g, width, 2 * n), F32),
                   jax.ShapeDtypeStruct((g, 2 * n, width), F32),
                   jax.ShapeDtypeStruct((g, ap_rows, 2 * n), F32)],
        compiler_params=_cparams("parallel"),
        name="s5_params",
    )(ldt.reshape(g, 1, 1), lr.reshape(g, n, 1), li.reshape(g, n, 1), lr.reshape(g, 1, n),
      li.reshape(g, 1, n), ct_re, ct_im, bt_re, bt_im, btt_re, btt_im)

    eye = jnp.eye(gpb, dtype=F32)
    m0_o = jnp.einsum("ogpmc,gh->ogpmhc", m0.reshape(n_oct, gpb, cw, lc, cw), eye)
    m0_o = m0_o.reshape(n_oct, lb, lc * lb).astype(BF16)
    e_o = jnp.einsum("ogtpx,gh->otgphx", e_g.reshape(n_oct, gpb, lc, cw, 2 * n), eye)
    e_o = e_o.reshape(n_oct, lc * lb, gpb * 2 * n).astype(BF16)
    p_o = jnp.einsum("ohxtc,hg->ohxtgc", p_g.reshape(n_oct, gpb, 2 * n, lc, cw), eye)
    p_o = p_o.reshape(n_oct, gpb * 2 * n, lc * lb).astype(BF16)
    ap_o = jnp.transpose(ap.reshape(n_oct, gpb, ap_rows, 2 * n), (0, 2, 1, 3))
    ap_o = ap_o.reshape(n_oct, ap_rows, gpb * 2 * n)

    u2 = u.reshape(rows // lc, lc * w)
    xspec = lambda t: pl.BlockSpec((n_chunks, lb), lambda o, b: (b, t * n_oct + o))
    o3 = lambda a: pl.BlockSpec((None,) + a.shape[1:], lambda o, b: (o, 0, 0))
    ys = pl.pallas_call(
        functools.partial(_s5_main_kernel, n_chunks=n_chunks, n_state=n),
        grid=(n_oct, batch),
        in_specs=[xspec(t) for t in range(lc)] + [o3(m0_o), o3(e_o), o3(p_o), o3(ap_o)],
        out_specs=pl.BlockSpec((lc, n_chunks, lb), lambda o, b: (0, b, o)),
        out_shape=jax.ShapeDtypeStruct((lc, rows // lc, w), F32),
        scratch_shapes=[pltpu.VMEM((lc * lb, lc * lb), BF16)],
        compiler_params=_cparams("parallel", "arbitrary"),
        name="s5_scan",
    )(*([u2] * lc), m0_o, e_o, p_o, ap_o)
    return jnp.swapaxes(ys, 0, 1).reshape(rows, w)


(_V_W0, _V_A0, _V_V0, _V_KK, _V_KA, _V_RK, _V_LNW, _V_LNB, _V_MUR, _V_MUK, _V_MUV) = range(11)
_VEC_ROWS = 16


def _softplus(z):
    return jnp.maximum(z, 0.0) + jnp.log(1.0 + jnp.exp(-jnp.abs(z)))


def _dot_nt(a, b, **kw):
    return lax.dot_general(a, b, (((1,), (1,)), ((), ())), preferred_element_type=F32, **kw)


def _dot_tn(a, b, **kw):
    return lax.dot_general(a, b, (((0,), (0,)), ((), ())), preferred_element_type=F32, **kw)


def _mm(a, b):
    return jnp.dot(a.astype(BF16), b.astype(BF16), preferred_element_type=F32)


def _rwkv_pairs(pairs):
    lc, pw = pairs[0][0].shape
    hd = RWKV_HEAD
    assert lc == hd and pw == 2 * hd
    lane = lax.broadcasted_iota(jnp.int32, (lc, pw), 1)
    row = lax.broadcasted_iota(jnp.int32, (lc, pw), 0)
    first = lane < hd
    col = jnp.where(first, lane, lane - hd)
    strict2 = row > col
    incl2 = row >= col
    ak_mask = jnp.logical_and(strict2, jnp.logical_not(first))
    zeros = jnp.zeros((lc, pw), F32)
    heads = [(p, e) for p in range(len(pairs)) for e in range(2)]

    gms = []
    for rt, at, bt, kt, v, bh, kh, wl, s0 in pairs:
        ar = jnp.concatenate([jnp.where(first, at, 0.0), jnp.where(first, rt, 0.0),
                              jnp.where(first, 0.0, at), jnp.where(first, 0.0, rt)], axis=0)
        gms.append(_dot_nt(ar.astype(BF16), jnp.concatenate([bt, kt], axis=0).astype(BF16)))
    ga = [gms[p][2 * e * lc:(2 * e + 1) * lc] for p, e in heads]
    gr = [gms[p][(2 * e + 1) * lc:(2 * e + 2) * lc] for p, e in heads]
    vz = [jnp.concatenate([zeros, pr[4]], axis=0).astype(BF16) for pr in pairs]
    zv = [jnp.concatenate([zeros, pr[4]], axis=1) for pr in pairs]

    nk = [jnp.where(strict2[:, :lc], g[:, :lc], 0.0).astype(BF16) for g in ga]
    y = [jnp.concatenate([pairs[p][1], _mm(jnp.where(ak_mask, ga[i], 0.0), vz[p])], axis=1)
         for i, (p, e) in enumerate(heads)]
    steps = max(1, (lc - 1).bit_length())
    for i in range(steps):
        y = [yy + _mm(nn, yy) for nn, yy in zip(nk, y)]
        if i + 1 < steps:
            nk = [_mm(nn, nn).astype(BF16) for nn in nk]
    rz = [_mm(jnp.where(incl2, gr[i], 0.0), jnp.concatenate([y[i], zv[p]], axis=0))
          for i, (p, e) in enumerate(heads)]

    x2s, z2s, uos = [], [], []
    for p, (rt, at, bt, kt, v, bh, kh, wl, s0) in enumerate(pairs):
        y0, y1, rz0, rz1 = y[2 * p], y[2 * p + 1], rz[2 * p], rz[2 * p + 1]
        x1 = jnp.where(first, y0[:, :pw], y1[:, :pw])
        z1 = rt + jnp.where(first, rz0[:, :pw], rz1[:, :pw])
        x2s.append(jnp.where(first, y0[:, pw:], y1[:, pw:]))
        z2s.append(jnp.where(first, rz0[:, pw:], rz1[:, pw:]))
        uos.append(_dot_nt(jnp.concatenate([x1, z1], axis=0).astype(BF16), s0.astype(BF16)))
    si = lax.broadcasted_iota(jnp.int32, (pw, pw), 0)
    sj = lax.broadcasted_iota(jnp.int32, (pw, pw), 1)
    same_head = (si < hd) == (sj < hd)
    outs = []
    for p, (rt, at, bt, kt, v, bh, kh, wl, s0) in enumerate(pairs):
        u = uos[p][:lc] + x2s[p]
        o = uos[p][lc:] + z2s[p]
        upd = _dot_tn(jnp.concatenate([u, v], axis=0).astype(BF16),
                      jnp.concatenate([bh, kh], axis=0).astype(BF16))
        outs.append((o, s0 * wl + jnp.where(same_head, upd, 0.0)))
    return outs


def _rwkv_kernel(*refs, has_vres, n_heads):
    if has_vres:
        (r_ref, k_ref, v_ref, l_ref, vf_ref, vec_ref, mul_ref, w2_ref, a2_ref, g2_ref, v2_ref,
         o_ref, s_sc, pr_sc, pk_sc, pv_sc, pl_sc, big_sc, on_sc, bonus_sc) = refs
        vout_ref = None
    else:
        (r_ref, k_ref, v_ref, l_ref, vec_ref, mul_ref, w2_ref, a2_ref, g2_ref,
         o_ref, vout_ref, s_sc, pr_sc, pk_sc, pv_sc, pl_sc, big_sc, on_sc, bonus_sc) = refs
    lc, width = r_ref.shape
    hd = RWKV_HEAD
    pw = 2 * hd

    @pl.when(pl.program_id(1) == 0)
    def _():
        s_sc[...] = jnp.zeros_like(s_sc)
        pr_sc[...] = jnp.zeros_like(pr_sc)
        pk_sc[...] = jnp.zeros_like(pk_sc)
        pv_sc[...] = jnp.zeros_like(pv_sc)
        pl_sc[...] = jnp.zeros_like(pl_sc)

    def vec(i):
        return vec_ref[i:i + 1, :]

    def shift_mix(x_ref, prev_sc, mu):
        x = x_ref[...]
        row = lax.broadcasted_iota(jnp.int32, x.shape, 0)
        prev = jnp.where(row == 0, prev_sc[...], pltpu.roll(x, 1, axis=0))
        prev_sc[...] = x[lc - 1:lc, :]
        return x + mu * (prev - x)

    xr = shift_mix(r_ref, pr_sc, vec(_V_MUR))
    xk = shift_mix(k_ref, pk_sc, vec(_V_MUK))
    xv = shift_mix(v_ref, pv_sc, vec(_V_MUV))
    xl = shift_mix(l_ref, pl_sc, mul_ref[...])
    xwa, xg, xvr = xl[:, 0:128], xl[:, 128:384], xl[:, 384:512]

    w_log = -_softplus(-(vec(_V_W0) + _mm(jnp.tanh(xwa), w2_ref[...]))) - 0.5
    logw = -jnp.exp(w_log)
    iclr = jax.nn.sigmoid(vec(_V_A0) + _mm(xwa, a2_ref[...]))
    gate = _mm(jax.nn.sigmoid(xg), g2_ref[...])
    if has_vres:
        v = xv + (vf_ref[...] - xv) * jax.nn.sigmoid(vec(_V_V0) + _mm(xvr, v2_ref[...]))
    else:
        v = xv
        vout_ref[...] = xv
    k = xk * (1.0 + (iclr - 1.0) * vec(_V_KA))

    ti = lax.broadcasted_iota(jnp.int32, (lc, lc), 0)
    tj = lax.broadcasted_iota(jnp.int32, (lc, lc), 1)
    cum = jnp.dot((ti >= tj).astype(F32), logw, preferred_element_type=F32, precision=HIGHEST)
    cum_last = cum[lc - 1:lc, :]
    dec_inv = jnp.exp(-cum)
    dec_tail = jnp.exp(cum_last - cum)

    big_sc[0] = xr * jnp.exp(cum)
    big_sc[1] = xk * vec(_V_KK)
    big_sc[2] = iclr
    big_sc[3] = k * dec_inv
    big_sc[4] = v
    big_sc[5] = jnp.exp(cum - logw)
    big_sc[6] = k * dec_tail
    big_sc[7] = xr * k * vec(_V_RK)
    big_sc[8] = dec_inv
    big_sc[9] = dec_tail
    big_sc[10] = jnp.broadcast_to(jnp.exp(cum_last), (lc, width))

    first = lax.broadcasted_iota(jnp.int32, (lc, pw), 1) < hd

    def head_sum(x):
        s0 = jnp.sum(jnp.where(first, x, 0.0), axis=-1, keepdims=True)
        s1 = jnp.sum(jnp.where(first, 0.0, x), axis=-1, keepdims=True)
        return jnp.where(first, s0, s1)

    n_pairs = n_heads // 2
    per_iter = min(RWKV_PAIR_UNROLL, n_pairs)

    def pair_group(g, carry):
        ids, ins, extra = [], [], []
        for j in range(per_iter):
            hp = g * per_iter + j
            off = hp * pw if isinstance(hp, int) else pl.multiple_of(hp * pw, pw)
            rt, kkraw, iclr_p, kt, v_p, dex, kh, rk, dinv, dtail, wl = [
                big_sc[i, :, pl.ds(off, pw)] for i in range(11)]
            kkn = kkraw * lax.rsqrt(head_sum(kkraw * kkraw) + 1e-12)
            b = kkn * iclr_p
            ids.append((hp, off))
            ins.append((rt, -kkn * dex, b * dinv, kt, v_p, b * dtail, kh, wl[0:1], s_sc[hp]))
            extra.append(head_sum(rk) * v_p)
        outs = _rwkv_pairs(ins)
        for (hp, off), (o, s_new), bonus in zip(ids, outs, extra):
            cen = o - head_sum(o) * (1.0 / hd)
            var = head_sum(cen * cen) * (1.0 / hd)
            s_sc[hp] = s_new
            on_sc[:, pl.ds(off, pw)] = cen * lax.rsqrt(var + LNX_EPS)
            bonus_sc[:, pl.ds(off, pw)] = bonus
        return carry

    if per_iter == n_pairs:
        pair_group(0, 0)
    else:
        lax.fori_loop(0, n_pairs // per_iter, pair_group, 0)
    o_ref[...] = (on_sc[...] * vec(_V_LNW) + vec(_V_LNB) + bonus_sc[...]) * gate


def _rwkv(p, vfirst, vecs, mu_l, w2, a2, g2p, v2p, *, batch, width, col0, lora_col):
    rows = p.shape[0]
    lc = RWKV_CHUNK
    tp = rows // batch
    nck = tp // lc
    n_heads = width // RWKV_HEAD
    has_vres = vfirst is not None
    rowblk = lambda b, c: b * nck + c
    cb = col0 // width
    tok = lambda j: pl.BlockSpec((lc, width), lambda b, c: (rowblk(b, c), j))
    full = lambda a: pl.BlockSpec(a.shape, lambda b, c: (0,) * a.ndim)
    in_specs = [tok(cb), tok(cb + 1), tok(cb + 2),
                pl.BlockSpec((lc, LORA_BLOCK), lambda b, c: (rowblk(b, c), lora_col // LORA_BLOCK))]
    args = [p, p, p, p]
    if has_vres:
        in_specs.append(tok(0))
        args.append(vfirst)
    in_specs += [full(vecs), full(mu_l), full(w2), full(a2), full(g2p)]
    args += [vecs, mu_l, w2, a2, g2p]
    if has_vres:
        in_specs.append(full(v2p))
        args.append(v2p)
    out_shape = [jax.ShapeDtypeStruct((rows, width), F32)]
    out_specs = [tok(0)]
    if not has_vres:
        out_shape.append(jax.ShapeDtypeStruct((rows, width), F32))
        out_specs.append(tok(0))
    outs = pl.pallas_call(
        functools.partial(_rwkv_kernel, has_vres=has_vres, n_heads=n_heads),
        grid=(batch, nck),
        in_specs=in_specs,
        out_specs=out_specs,
        out_shape=out_shape,
        scratch_shapes=[pltpu.VMEM((n_heads // 2, 2 * RWKV_HEAD, 2 * RWKV_HEAD), F32),
                        pltpu.VMEM((1, width), F32), pltpu.VMEM((1, width), F32),
                        pltpu.VMEM((1, width), F32), pltpu.VMEM((1, LORA_BLOCK), F32),
                        pltpu.VMEM((11, lc, width), F32),
                        pltpu.VMEM((lc, width), F32), pltpu.VMEM((lc, width), F32)],
        compiler_params=_cparams("parallel", "arbitrary"),
        name="rwkv7",
    )(*args)
    return outs if not has_vres else (outs[0], None)


def _merge_kernel(x_ref, ga_ref, gb_ref, u_ref, ys_ref, yb_ref, d_ref, wglu_ref, wua_ref, wub_ref,
                  wo_ref, o_ref):
    y = ys_ref[...] + d_ref[...] * u_ref[...]
    y = jax.nn.gelu(y, approximate=True)
    ya = y * jax.nn.sigmoid(jnp.dot(y.astype(BF16), wglu_ref[...], preferred_element_type=F32))
    up_a = jnp.dot(ya.astype(BF16), wua_ref[...], preferred_element_type=F32)
    up_b = jnp.dot(yb_ref[...].astype(BF16), wub_ref[...], preferred_element_type=F32)
    merged = jax.nn.sigmoid(ga_ref[...]) * up_a + jax.nn.sigmoid(gb_ref[...]) * up_b
    o_ref[...] = x_ref[...] + jnp.dot(merged.astype(BF16), wo_ref[...], preferred_element_type=F32)


def _merge(x, p, ys, yb, d_skip, wglu, wua, wub, wo, *, u_col):
    rows, d = x.shape
    w = ys.shape[1]
    tm = min(256, _row_tile(rows))
    row = lambda width, j: pl.BlockSpec((tm, width), lambda i: (i, j))
    const = lambda a: pl.BlockSpec(a.shape, lambda i: (0, 0), pipeline_mode=pl.Buffered(1))
    return pl.pallas_call(
        _merge_kernel,
        grid=(rows // tm,),
        in_specs=[row(d, 0), row(d, 0), row(d, 1), row(w, u_col // w), row(w, 0), row(w, 0),
                  const(d_skip), const(wglu), const(wua), const(wub), const(wo)],
        out_specs=row(d, 0),
        out_shape=jax.ShapeDtypeStruct((rows, d), F32),
        compiler_params=_cparams("parallel"),
        name="merge",
    )(x, p, p, p, ys, yb, d_skip, wglu, wua, wub, wo)


def _norm_kernel(x_ref, g_ref, o_ref):
    o_ref[...] = _rms(x_ref[...], g_ref[...])


def _final_norm(x, g):
    rows, d = x.shape
    tm = _row_tile(rows)
    return pl.pallas_call(
        _norm_kernel,
        grid=(rows // tm,),
        in_specs=[pl.BlockSpec((tm, d), lambda i: (i, 0)), pl.BlockSpec((1, d), lambda i: (0, 0))],
        out_specs=pl.BlockSpec((tm, d), lambda i: (i, 0)),
        out_shape=jax.ShapeDtypeStruct((rows, d), F32),
        compiler_params=_cparams("parallel"),
        name="final_norm",
    )(x, g)


def _pad_cols(a, width):
    return jnp.pad(a, ((0, 0), (0, width - a.shape[1])))


def _pad_rows(a, height):
    return jnp.pad(a, ((0, height - a.shape[0]), (0, 0)))


def kernel(x, meta_tokens, ffn1_norm, ffn1_w_gate, ffn1_w_up, ffn1_w_down, mix_norm, w_in_first, w_in_rest, mu_shift, mu_vres, ssm_lambda_re, ssm_lambda_im, ssm_log_dt, ssm_b_re, ssm_b_im, ssm_c_re, ssm_c_im, ssm_d, ssm_w_glu, rwkv_w0, rwkv_w2, rwkv_a0, rwkv_a2, rwkv_v0, rwkv_v2, rwkv_g2, rwkv_k_k, rwkv_k_a, rwkv_r_k, rwkv_lnx_w, rwkv_lnx_b, w_up_ssm, w_up_rwkv, w_out, ffn2_norm, ffn2_w_gate, ffn2_w_up, ffn2_w_down, final_norm):
    bsz, seq, d = x.shape
    n_meta = meta_tokens.shape[0]
    depth = ffn1_norm.shape[0]
    w = ssm_d.shape[1]
    n_groups = ssm_b_re.shape[1]
    ld, la, lg, lv = rwkv_w2.shape[1], rwkv_a2.shape[1], rwkv_g2.shape[1], rwkv_v2.shape[1]
    assert w == rwkv_w0.shape[1] and 2 * w == d and w % 128 == 0
    assert n_groups * SSM_GROUP == w and ld + la <= 128 and lg <= 256 and lv <= 128
    t_len = n_meta + seq
    tp = -(-t_len // SEQ_ALIGN) * SEQ_ALIGN
    rows = bsz * tp

    meta = jnp.broadcast_to(meta_tokens[None].astype(x.dtype), (bsz, n_meta, d))
    h_res = jnp.concatenate([meta, x, jnp.zeros((bsz, tp - t_len, d), x.dtype)], axis=1)
    h_res = h_res.reshape(rows, d)

    u_col = 2 * d
    rkv_col = u_col + w
    lora_col = rkv_col + 3 * w
    c_w = 4 * w
    c_g = c_w + ld + la
    c_ga = c_g + lg
    c_gb = c_ga + d
    p_common = c_gb + d

    v_first = None
    for i in range(depth):
        h_res = _ffn(h_res, ffn1_norm[i][None], ffn1_w_gate[i].astype(BF16),
                     ffn1_w_up[i].astype(BF16), ffn1_w_down[i].astype(BF16))

        w_in = w_in_first if i == 0 else w_in_rest[i - 1]
        blocks = [w_in[:, c_ga:c_gb], w_in[:, c_gb:p_common], w_in[:, :c_w],
                  _pad_cols(w_in[:, c_w:c_g], 128), _pad_cols(w_in[:, c_g:c_ga], 256),
                  _pad_cols(w_in[:, p_common:], 128)]
        p = _proj(h_res, mix_norm[i][None], jnp.concatenate(blocks, axis=1).astype(BF16))

        ys = _s5_scan(p[:, u_col:u_col + w], ssm_log_dt[i], ssm_lambda_re[i], ssm_lambda_im[i],
                      ssm_c_re[i], ssm_c_im[i], ssm_b_re[i], ssm_b_im[i], batch=bsz)

        mu = mu_shift[i]
        zero_w = jnp.zeros((w,), F32)
        vecs = jnp.stack([rwkv_w0[i], rwkv_a0[i], rwkv_v0[i - 1] if i else zero_w, rwkv_k_k[i],
                          rwkv_k_a[i], rwkv_r_k[i].reshape(w), rwkv_lnx_w[i], rwkv_lnx_b[i],
                          mu[0:w], mu[w:2 * w], mu[2 * w:3 * w]])
        vecs = _pad_rows(vecs, _VEC_ROWS)
        mu_v = mu_vres[i - 1] if i else jnp.zeros((lv,), F32)
        mu_l = jnp.concatenate([jnp.pad(mu[3 * w:3 * w + ld + la], (0, 128 - ld - la)),
                                jnp.pad(mu[3 * w + ld + la:], (0, 256 - lg)),
                                jnp.pad(mu_v, (0, 128 - lv))])[None]
        w2p = _pad_rows(rwkv_w2[i], 128).astype(BF16)
        a2p = jnp.pad(rwkv_a2[i], ((ld, 128 - ld - la), (0, 0))).astype(BF16)
        g2p = _pad_rows(rwkv_g2[i], 256).astype(BF16)
        v2p = _pad_rows(rwkv_v2[i - 1], 128).astype(BF16) if i else None
        yb, v_new = _rwkv(p, v_first, vecs, mu_l, w2p, a2p, g2p, v2p,
                          batch=bsz, width=w, col0=rkv_col, lora_col=lora_col)
        if i == 0:
            v_first = v_new

        h_res = _merge(h_res, p, ys, yb, ssm_d[i][None], ssm_w_glu[i].astype(BF16),
                       w_up_ssm[i].astype(BF16), w_up_rwkv[i].astype(BF16), w_out[i].astype(BF16),
                       u_col=u_col)

        h_res = _ffn(h_res, ffn2_norm[i][None], ffn2_w_gate[i].astype(BF16),
                     ffn2_w_up[i].astype(BF16), ffn2_w_down[i].astype(BF16))

    out = _final_norm(h_res, final_norm[None]).reshape(bsz, tp, d)
    return out[:, n_meta:t_len]
```

```python
import functools
import math

import jax
import jax.numpy as jnp
from jax import lax
from jax.experimental import pallas as pl
from jax.experimental.pallas import tpu as pltpu

F32 = jnp.float32
BF16 = jnp.bfloat16
HIGHEST = lax.Precision.HIGHEST

NORM_EPS = 1e-6
RWKV_HEAD = 64
LNX_EPS = RWKV_HEAD * 1e-5
SSM_GROUP = 16
S5_CHUNK = 16
RWKV_CHUNK = 64
RWKV_SUBCHUNKS = 3
PROJ_ROW_TILE = 1536
LORA_BLOCK = 512
SEQ_ALIGN = 128
VMEM_LIMIT_BYTES = 56 * 1024 * 1024


def _cparams(*sem):
    return pltpu.CompilerParams(dimension_semantics=sem, vmem_limit_bytes=VMEM_LIMIT_BYTES)


def _rms(x, g):
    inv = lax.rsqrt(jnp.mean(x * x, axis=-1, keepdims=True) + NORM_EPS)
    return (x * inv) * g


def _row_tile(rows):
    for tm in (512, 256, 128, 64, 32, 16, 8):
        if rows % tm == 0:
            return tm
    raise ValueError(rows)


def _col_tile(cols):
    for tn in (512, 256, 128):
        if cols % tn == 0:
            return tn
    raise ValueError(cols)


def _ffn_kernel(x_ref, g_ref, wg_ref, wu_ref, wd_ref, o_ref, h_sc):
    f = pl.program_id(1)

    @pl.when(f == 0)
    def _():
        x = x_ref[...]
        h_sc[...] = _rms(x, g_ref[...]).astype(BF16)
        o_ref[...] = x

    h = h_sc[...]
    a = jnp.dot(h, wg_ref[...], preferred_element_type=F32)
    b = jnp.dot(h, wu_ref[...], preferred_element_type=F32)
    z = (0.5 * (a * jax.nn.sigmoid(a)) * b).astype(BF16)
    o_ref[...] += jnp.dot(z, wd_ref[...], preferred_element_type=F32)


def _ffn(x, g, wg, wu, wd):
    rows, d = x.shape
    dff = wg.shape[1]
    tm = _row_tile(rows)
    tf = _col_tile(dff)
    return pl.pallas_call(
        _ffn_kernel,
        grid=(rows // tm, dff // tf),
        in_specs=[
            pl.BlockSpec((tm, d), lambda i, f: (i, 0)),
            pl.BlockSpec((1, d), lambda i, f: (0, 0)),
            pl.BlockSpec((d, tf), lambda i, f: (0, f)),
            pl.BlockSpec((d, tf), lambda i, f: (0, f)),
            pl.BlockSpec((tf, d), lambda i, f: (f, 0)),
        ],
        out_specs=pl.BlockSpec((tm, d), lambda i, f: (i, 0)),
        out_shape=jax.ShapeDtypeStruct((rows, d), F32),
        scratch_shapes=[pltpu.VMEM((tm, d), BF16)],
        compiler_params=_cparams("parallel", "arbitrary"),
        name="ffn",
    )(x, g, wg, wu, wd)


def _proj_kernel(x_ref, g_ref, w_ref, o_ref, h_sc):
    @pl.when(pl.program_id(1) == 0)
    def _():
        h_sc[...] = _rms(x_ref[...], g_ref[...]).astype(BF16)

    o_ref[...] = jnp.dot(h_sc[...], w_ref[...], preferred_element_type=F32)


def _proj_split_kernel(x_ref, g_ref, w_ref, o_ref, u_ref, h_sc, *, j0, nu):
    _proj_kernel(x_ref, g_ref, w_ref, o_ref, h_sc)
    j = pl.program_id(1)

    @pl.when(jnp.logical_and(j >= j0, j < j0 + nu))
    def _():
        u_ref[...] = o_ref[...]


def _proj(x, g, w, u_col, u_width):
    rows, d = x.shape
    n = w.shape[1]
    tm = PROJ_ROW_TILE if rows % PROJ_ROW_TILE == 0 else _row_tile(rows)
    tn = _col_tile(n)
    split = u_col % tn == 0 and u_width % tn == 0
    in_specs = [
        pl.BlockSpec((tm, d), lambda i, j: (i, 0)),
        pl.BlockSpec((1, d), lambda i, j: (0, 0)),
        pl.BlockSpec((d, tn), lambda i, j: (0, j)),
    ]
    p_spec = pl.BlockSpec((tm, tn), lambda i, j: (i, j))
    p_shape = jax.ShapeDtypeStruct((rows, n), F32)
    common = dict(grid=(rows // tm, n // tn), in_specs=in_specs,
                  scratch_shapes=[pltpu.VMEM((tm, d), BF16)],
                  compiler_params=_cparams("parallel", "arbitrary"), name="in_proj")
    if not split:
        p = pl.pallas_call(_proj_kernel, out_specs=p_spec, out_shape=p_shape, **common)(x, g, w)
        return p, p[:, u_col:u_col + u_width]
    j0, nu = u_col // tn, u_width // tn
    u_spec = pl.BlockSpec((tm, tn), lambda i, j: (i, jnp.clip(j - j0, 0, nu - 1)))
    return pl.pallas_call(
        functools.partial(_proj_split_kernel, j0=j0, nu=nu),
        out_specs=[p_spec, u_spec],
        out_shape=[p_shape, jax.ShapeDtypeStruct((rows, u_width), F32)],
        **common)(x, g, w)


def _cpow(lr, li, dt, m):
    mag = jnp.exp(lr * dt * m)
    ang = li * dt * m
    return mag * jnp.cos(ang), mag * jnp.sin(ang)


def _s5_param_kernel(ldt_ref, lrc_ref, lic_ref, lrr_ref, lir_ref, ctre_ref, ctim_ref,
                     btre_ref, btim_ref, bttre_ref, bttim_ref, m0_ref, e_ref, p_ref, ap_ref,
                     *, n_chunks):
    lc = S5_CHUNK
    cw = SSM_GROUP
    width = lc * cw
    dt = jnp.exp(ldt_ref[...])
    lr_c, li_c = lrc_ref[...], lic_ref[...]
    lr_r, li_r = lrr_ref[...], lir_ref[...]

    are, aim = _cpow(lr_r, li_r, dt, 1.0)
    den = lr_r * lr_r + li_r * li_r
    nr, ni = are - 1.0, aim
    q_re = (nr * lr_r + ni * li_r) / den
    q_im = (ni * lr_r - nr * li_r) / den

    lag = (lax.broadcasted_iota(jnp.int32, (1, width), 1) // cw).astype(F32)
    ct_re, ct_im = ctre_ref[...], ctim_ref[...]
    p_re, p_im = _cpow(lr_c, li_c, dt, lag)
    d0_re = ct_re * p_re - ct_im * p_im
    d0_im = ct_re * p_im + ct_im * p_re
    bt_re, bt_im = btre_ref[...], btim_ref[...]
    bbt_re = q_re * bt_re - q_im * bt_im
    bbt_im = q_re * bt_im + q_im * bt_re
    m0_ref[...] = (jnp.dot(bbt_re, d0_re, preferred_element_type=F32, precision=HIGHEST)
                   - jnp.dot(bbt_im, d0_im, preferred_element_type=F32, precision=HIGHEST))

    rowj = (lax.broadcasted_iota(jnp.int32, (width, 1), 0) // cw).astype(F32)
    e_pre, e_pim = _cpow(lr_r, li_r, dt, (lc - 1.0) - rowj)
    btt_re, btt_im = bttre_ref[...], bttim_ref[...]
    bb2_re = q_re * btt_re - q_im * btt_im
    bb2_im = q_re * btt_im + q_im * btt_re
    e_ref[...] = jnp.concatenate([bb2_re * e_pre - bb2_im * e_pim,
                                  bb2_re * e_pim + bb2_im * e_pre], axis=1)

    p1_re, p1_im = _cpow(lr_c, li_c, dt, lag + 1.0)
    p_ref[...] = jnp.concatenate([ct_re * p1_re - ct_im * p1_im,
                                  -(ct_re * p1_im + ct_im * p1_re)], axis=0)

    ap_ref[...] = jnp.zeros_like(ap_ref)
    for i, shift in enumerate(_scan_shifts(n_chunks)):
        pre, pim = _cpow(lr_r, li_r, dt, float(lc * shift))
        ap_ref[2 * i:2 * i + 1, :] = jnp.concatenate([pre, pre], axis=1)
        ap_ref[2 * i + 1:2 * i + 2, :] = jnp.concatenate([-pim, pim], axis=1)


def _scan_shifts(n_chunks):
    shifts, s = [], 1
    while s < n_chunks:
        shifts.append(s)
        s *= 2
    return shifts


def _s5_main_kernel(*refs, n_chunks, n_state):
    lc = S5_CHUNK
    cw = SSM_GROUP
    x_refs = refs[:lc]
    m0_ref, e_ref, p_ref, ap_ref, y_ref, m_sc, e_sc, p_sc, ap_sc = refs[lc:]
    lb = x_refs[0].shape[1]
    gpb = lb // cw
    width = lc * cw
    sw = 2 * n_state

    @pl.when(pl.program_id(1) == 0)
    def _():
        def iota(shape, dim):
            return lax.broadcasted_iota(jnp.int32, shape, dim)

        def spread(shape, dim_small):
            small, big = iota(shape, dim_small), iota(shape, 1 - dim_small)
            return jnp.logical_and(small // cw == big // lb, small % cw == big % cw)

        lane_expand = spread((width, lc * lb), 0).astype(BF16)
        row_expand = spread((lc * lb, width), 1).astype(BF16)

        m0w = jnp.dot(m0_ref[...].astype(BF16), lane_expand, preferred_element_type=F32)
        own = iota(m0w.shape, 0) // cw == (iota(m0w.shape, 1) % lb) // cw
        m0 = jnp.where(own, m0w, 0.0).astype(BF16)
        for t in range(lc):
            blk = m0 if t == 0 else jnp.concatenate(
                [jnp.zeros((lb, t * lb), m0.dtype), m0[:, :(lc - t) * lb]], axis=1)
            m_sc[t * lb:(t + 1) * lb, :] = blk

        e_stack = jnp.concatenate([e_ref[g] for g in range(gpb)], axis=1).astype(BF16)
        e_full = jnp.dot(row_expand, e_stack, preferred_element_type=F32)
        own = (iota(e_full.shape, 0) % lb) // cw == iota(e_full.shape, 1) // sw
        e_sc[...] = jnp.where(own, e_full, 0.0).astype(BF16)

        p_stack = p_ref[...].reshape(gpb * sw, width).astype(BF16)
        p_full = jnp.dot(p_stack, lane_expand, preferred_element_type=F32)
        own = iota(p_full.shape, 0) // sw == (iota(p_full.shape, 1) % lb) // cw
        p_sc[...] = jnp.where(own, p_full, 0.0).astype(BF16)

        ap_sc[...] = jnp.concatenate([ap_ref[g] for g in range(gpb)], axis=1)

    x = jnp.concatenate([r[...].astype(BF16) for r in x_refs], axis=1)
    y = jnp.dot(x, m_sc[...], preferred_element_type=F32)
    s = jnp.dot(x, e_sc[...], preferred_element_type=F32)

    def swap_re_im(a):
        return jnp.concatenate([pltpu.roll(a[:, j * sw:(j + 1) * sw], n_state, axis=1)
                                for j in range(a.shape[1] // sw)], axis=1)

    crow = lax.broadcasted_iota(jnp.int32, (n_chunks, 1), 0)
    for i, shift in enumerate(_scan_shifts(n_chunks)):
        prev = jnp.where(crow >= shift, pltpu.roll(s, shift, axis=0), 0.0)
        s = s + ap_sc[2 * i:2 * i + 1, :] * prev + ap_sc[2 * i + 1:2 * i + 2, :] * swap_re_im(prev)
    s_start = jnp.where(crow >= 1, pltpu.roll(s, 1, axis=0), 0.0)
    y = y + jnp.dot(s_start.astype(BF16), p_sc[...], preferred_element_type=F32)
    for t in range(lc):
        y_ref[t] = y[:, t * lb:(t + 1) * lb]


def _s5_scan(u, ldt, lr, li, c_re, c_im, b_re, b_im, *, batch):
    rows, w = u.shape
    g, n = lr.shape
    cw = SSM_GROUP
    lc = S5_CHUNK
    width = lc * cw
    lb = 128
    gpb = lb // cw
    n_oct = w // lb
    n_chunks = rows // batch // lc
    ap_rows = -(-2 * len(_scan_shifts(n_chunks)) // 8) * 8
    ct_re = jnp.tile(jnp.swapaxes(c_re, 1, 2), (1, 1, lc))
    ct_im = jnp.tile(jnp.swapaxes(c_im, 1, 2), (1, 1, lc))
    bt_re = jnp.swapaxes(b_re, 1, 2)
    bt_im = jnp.swapaxes(b_im, 1, 2)
    btt_re = jnp.tile(bt_re, (1, lc, 1))
    btt_im = jnp.tile(bt_im, (1, lc, 1))
    g3 = lambda a, b: pl.BlockSpec((None, a, b), lambda i: (i, 0, 0))
    m0, e_g, p_g, ap = pl.pallas_call(
        functools.partial(_s5_param_kernel, n_chunks=n_chunks),
        grid=(g,),
        in_specs=[g3(1, 1), g3(n, 1), g3(n, 1), g3(1, n), g3(1, n),
                  g3(n, width), g3(n, width), g3(cw, n), g3(cw, n), g3(width, n), g3(width, n)],
        out_specs=[g3(cw, width), g3(width, 2 * n), g3(2 * n, width), g3(ap_rows, 2 * n)],
        out_shape=[jax.ShapeDtypeStruct((g, cw, width), F32),
                   jax.ShapeDtypeStruct((g, width, 2 * n), F32),
                   jax.ShapeDtypeStruct((g, 2 * n, width), F32),
                   jax.ShapeDtypeStruct((g, ap_rows, 2 * n), F32)],
        compiler_params=_cparams("parallel"),
        name="s5_params",
    )(ldt.reshape(g, 1, 1), lr.reshape(g, n, 1), li.reshape(g, n, 1), lr.reshape(g, 1, n),
      li.reshape(g, 1, n), ct_re, ct_im, bt_re, bt_im, btt_re, btt_im)

    m0_o = m0.reshape(n_oct, lb, width)
    e_o = e_g.reshape(n_oct, gpb, width, 2 * n)
    p_o = p_g.reshape(n_oct, gpb, 2 * n, width)
    ap_o = ap.reshape(n_oct, gpb, ap_rows, 2 * n)

    u2 = u.reshape(rows // lc, lc * w)
    xspec = lambda t: pl.BlockSpec((n_chunks, lb), lambda o, b: (b, t * n_oct + o))
    per_oct = lambda a: pl.BlockSpec((None,) + a.shape[1:], lambda o, b: (o,) + (0,) * (a.ndim - 1))
    ys = pl.pallas_call(
        functools.partial(_s5_main_kernel, n_chunks=n_chunks, n_state=n),
        grid=(n_oct, batch),
        in_specs=[xspec(t) for t in range(lc)] + [per_oct(m0_o), per_oct(e_o), per_oct(p_o),
                                                   per_oct(ap_o)],
        out_specs=pl.BlockSpec((lc, n_chunks, lb), lambda o, b: (0, b, o)),
        out_shape=jax.ShapeDtypeStruct((lc, rows // lc, w), F32),
        scratch_shapes=[pltpu.VMEM((lc * lb, lc * lb), BF16),
                        pltpu.VMEM((lc * lb, gpb * 2 * n), BF16),
                        pltpu.VMEM((gpb * 2 * n, lc * lb), BF16),
                        pltpu.VMEM((ap_rows, gpb * 2 * n), F32)],
        compiler_params=_cparams("parallel", "arbitrary"),
        name="s5_scan",
    )(*([u2] * lc), m0_o, e_o, p_o, ap_o)
    return jnp.swapaxes(ys, 0, 1).reshape(rows, w)


(_V_W0, _V_A0, _V_V0, _V_KK, _V_KA, _V_RK, _V_LNW, _V_LNB, _V_MUR, _V_MUK, _V_MUV) = range(11)
_VEC_ROWS = 16


def _softplus(z):
    return jnp.maximum(z, 0.0) + jnp.log(1.0 + jnp.exp(-jnp.abs(z)))


def _dot_nt(a, b, **kw):
    return lax.dot_general(a, b, (((1,), (1,)), ((), ())), preferred_element_type=F32, **kw)


def _dot_tn(a, b, **kw):
    return lax.dot_general(a, b, (((0,), (0,)), ((), ())), preferred_element_type=F32, **kw)


def _mm(a, b):
    return jnp.dot(a.astype(BF16), b.astype(BF16), preferred_element_type=F32)


def _rwkv_pairs(pairs):
    lc, pw = pairs[0][0].shape
    hd = RWKV_HEAD
    assert lc == hd and pw == 2 * hd
    lane = lax.broadcasted_iota(jnp.int32, (lc, pw), 1)
    row = lax.broadcasted_iota(jnp.int32, (lc, pw), 0)
    first = lane < hd
    col = jnp.where(first, lane, lane - hd)
    strict2 = row > col
    incl2 = row >= col
    ak_mask = jnp.logical_and(strict2, jnp.logical_not(first))
    zeros = jnp.zeros((lc, pw), F32)
    heads = [(p, e) for p in range(len(pairs)) for e in range(2)]

    gms = []
    for rt, at, bt, kt, v, bh, kh, wl, s0 in pairs:
        ar = jnp.concatenate([jnp.where(first, at, 0.0), jnp.where(first, rt, 0.0),
                              jnp.where(first, 0.0, at), jnp.where(first, 0.0, rt)], axis=0)
        gms.append(_dot_nt(ar.astype(BF16), jnp.concatenate([bt, kt], axis=0).astype(BF16)))
    ga = [gms[p][2 * e * lc:(2 * e + 1) * lc] for p, e in heads]
    gr = [gms[p][(2 * e + 1) * lc:(2 * e + 2) * lc] for p, e in heads]
    vr = [pltpu.roll(pr[4], hd, axis=1) for pr in pairs]
    vz = [jnp.concatenate([zeros, r], axis=0).astype(BF16) for r in vr]

    nk = [jnp.where(strict2[:, :lc], g[:, :lc], 0.0).astype(BF16) for g in ga]
    y = []
    for i, (p, e) in enumerate(heads):
        akv = _mm(jnp.where(ak_mask, ga[i], 0.0), vz[p])
        y.append(jnp.where(first, pairs[p][1], akv) if e == 0 else jnp.where(first, akv, pairs[p][1]))
    steps = max(1, (lc - 1).bit_length())
    for i in range(steps):
        y = [yy + _mm(nn, yy) for nn, yy in zip(nk, y)]
        if i + 1 < steps:
            nk = [_mm(nn, nn).astype(BF16) for nn in nk]
    rz = []
    for i, (p, e) in enumerate(heads):
        zv = jnp.where(first, 0.0, vr[p]) if e == 0 else jnp.where(first, vr[p], 0.0)
        rz.append(_mm(jnp.where(incl2, gr[i], 0.0), jnp.concatenate([y[i], zv], axis=0)))

    x2s, z2s, uos = [], [], []
    for p, (rt, at, bt, kt, v, bh, kh, wl, s0) in enumerate(pairs):
        y0, y1, rz0, rz1 = y[2 * p], y[2 * p + 1], rz[2 * p], rz[2 * p + 1]
        x1 = jnp.where(first, y0, y1)
        z1 = rt + jnp.where(first, rz0, rz1)
        x2s.append(pltpu.roll(jnp.where(first, y1, y0), hd, axis=1))
        z2s.append(pltpu.roll(jnp.where(first, rz1, rz0), hd, axis=1))
        uos.append(_dot_nt(jnp.concatenate([x1, z1], axis=0).astype(BF16), s0.astype(BF16)))
    si = lax.broadcasted_iota(jnp.int32, (pw, pw), 0)
    sj = lax.broadcasted_iota(jnp.int32, (pw, pw), 1)
    same_head = (si < hd) == (sj < hd)
    outs = []
    for p, (rt, at, bt, kt, v, bh, kh, wl, s0) in enumerate(pairs):
        u = uos[p][:lc] + x2s[p]
        o = uos[p][lc:] + z2s[p]
        upd = _dot_tn(jnp.concatenate([u, v], axis=0).astype(BF16),
                      jnp.concatenate([bh, kh], axis=0).astype(BF16))
        outs.append((o, s0 * wl + jnp.where(same_head, upd, 0.0)))
    return outs


def _rwkv_kernel(*refs, has_vres, n_heads):
    if has_vres:
        (r_ref, k_ref, v_ref, l_ref, vf_ref, vec_ref, mul_ref, w2_ref, a2_ref, g2_ref, v2_ref,
         o_ref, s_sc, pr_sc, pk_sc, pv_sc, pl_sc, big_sc, on_sc, bonus_sc, gate_sc) = refs
        vout_ref = None
    else:
        (r_ref, k_ref, v_ref, l_ref, vec_ref, mul_ref, w2_ref, a2_ref, g2_ref,
         o_ref, vout_ref, s_sc, pr_sc, pk_sc, pv_sc, pl_sc, big_sc, on_sc, bonus_sc, gate_sc) = refs
    nr, width = r_ref.shape
    lc = RWKV_CHUNK
    n_sub = nr // lc
    hd = RWKV_HEAD
    pw = 2 * hd

    @pl.when(pl.program_id(1) == 0)
    def _():
        s_sc[...] = jnp.zeros_like(s_sc)
        pr_sc[...] = jnp.zeros_like(pr_sc)
        pk_sc[...] = jnp.zeros_like(pk_sc)
        pv_sc[...] = jnp.zeros_like(pv_sc)
        pl_sc[...] = jnp.zeros_like(pl_sc)

    def vec(i):
        return vec_ref[i:i + 1, :]

    def shift_mix(x_ref, prev_sc, mu):
        x = x_ref[...]
        row = lax.broadcasted_iota(jnp.int32, x.shape, 0)
        prev = jnp.where(row == 0, prev_sc[...], pltpu.roll(x, 1, axis=0))
        prev_sc[...] = x[nr - 1:nr, :]
        return x + mu * (prev - x)

    xr = shift_mix(r_ref, pr_sc, vec(_V_MUR))
    xk = shift_mix(k_ref, pk_sc, vec(_V_MUK))
    xv = shift_mix(v_ref, pv_sc, vec(_V_MUV))
    xl = shift_mix(l_ref, pl_sc, mul_ref[...])
    xwa, xg, xvr = xl[:, 0:128], xl[:, 128:384], xl[:, 384:512]

    w_log = -_softplus(-(vec(_V_W0) + _mm(jnp.tanh(xwa), w2_ref[...]))) - 0.5
    logw = -jnp.exp(w_log)
    iclr = jax.nn.sigmoid(vec(_V_A0) + _mm(xwa, a2_ref[...]))
    gate_sc[...] = _mm(jax.nn.sigmoid(xg), g2_ref[...])
    if has_vres:
        v = xv + (vf_ref[...] - xv) * jax.nn.sigmoid(vec(_V_V0) + _mm(xvr, v2_ref[...]))
    else:
        v = xv
        vout_ref[...] = xv
    k = xk * (1.0 + (iclr - 1.0) * vec(_V_KA))

    ti = lax.broadcasted_iota(jnp.int32, (nr, nr), 0)
    tj = lax.broadcasted_iota(jnp.int32, (nr, nr), 1)
    tri = jnp.logical_and(ti >= tj, ti // lc == tj // lc)
    cum = jnp.dot(tri.astype(F32), logw, preferred_element_type=F32, precision=HIGHEST)
    sub = lax.broadcasted_iota(jnp.int32, (nr, 1), 0) // lc
    cum_last = cum[lc - 1:lc, :]
    for h in range(1, n_sub):
        cum_last = jnp.where(sub == h, cum[(h + 1) * lc - 1:(h + 1) * lc, :], cum_last)
    dec_inv = jnp.exp(-cum)
    dec_tail = jnp.exp(cum_last - cum)

    big_sc[0] = xr * jnp.exp(cum)
    big_sc[1] = xk * vec(_V_KK)
    big_sc[2] = iclr
    big_sc[3] = k * dec_inv
    big_sc[4] = v
    big_sc[5] = jnp.exp(cum - logw)
    big_sc[6] = k * dec_tail
    big_sc[7] = xr * k * vec(_V_RK)
    big_sc[8] = dec_inv
    big_sc[9] = dec_tail
    big_sc[10] = jnp.broadcast_to(jnp.exp(cum_last), (nr, width))

    first = lax.broadcasted_iota(jnp.int32, (lc, pw), 1) < hd

    def head_sum(x):
        s0 = jnp.sum(jnp.where(first, x, 0.0), axis=-1, keepdims=True)
        s1 = jnp.sum(jnp.where(first, 0.0, x), axis=-1, keepdims=True)
        return jnp.where(first, s0, s1)

    n_pairs = n_heads // 2
    states = [s_sc[hp] for hp in range(n_pairs)]
    for h in range(n_sub):
        rows = slice(h * lc, (h + 1) * lc)
        ins, extra = [], []
        for hp in range(n_pairs):
            lanes = slice(hp * pw, (hp + 1) * pw)
            rt, kkraw, iclr_p, kt, v_p, dex, kh, rk, dinv, dtail, wl = [
                big_sc[i, rows, lanes] for i in range(11)]
            kkn = kkraw * lax.rsqrt(head_sum(kkraw * kkraw) + 1e-12)
            b = kkn * iclr_p
            ins.append((rt, -kkn * dex, b * dinv, kt, v_p, b * dtail, kh, wl[0:1], states[hp]))
            extra.append(head_sum(rk) * v_p)
        outs = _rwkv_pairs(ins)
        states = [s_new for _, s_new in outs]
        for hp, ((o, _), bonus) in enumerate(zip(outs, extra)):
            lanes = slice(hp * pw, (hp + 1) * pw)
            cen = o - head_sum(o) * (1.0 / hd)
            var = head_sum(cen * cen) * (1.0 / hd)
            on_sc[rows, lanes] = cen * lax.rsqrt(var + LNX_EPS)
            bonus_sc[rows, lanes] = bonus
    for hp in range(n_pairs):
        s_sc[hp] = states[hp]
    o_ref[...] = (on_sc[...] * vec(_V_LNW) + vec(_V_LNB) + bonus_sc[...]) * gate_sc[...]


def _rwkv(p, vfirst, vecs, mu_l, w2, a2, g2p, v2p, *, batch, width, col0, lora_col):
    rows = p.shape[0]
    tp = rows // batch
    n_sub = RWKV_SUBCHUNKS if tp % (RWKV_SUBCHUNKS * RWKV_CHUNK) == 0 else 1
    lc = n_sub * RWKV_CHUNK
    nck = tp // lc
    n_heads = width // RWKV_HEAD
    has_vres = vfirst is not None
    rowblk = lambda b, c: b * nck + c
    cb = col0 // width
    tok = lambda j: pl.BlockSpec((lc, width), lambda b, c: (rowblk(b, c), j))
    full = lambda a: pl.BlockSpec(a.shape, lambda b, c: (0,) * a.ndim)
    in_specs = [tok(cb), tok(cb + 1), tok(cb + 2),
                pl.BlockSpec((lc, LORA_BLOCK), lambda b, c: (rowblk(b, c), lora_col // LORA_BLOCK))]
    args = [p, p, p, p]
    if has_vres:
        in_specs.append(tok(0))
        args.append(vfirst)
    in_specs += [full(vecs), full(mu_l), full(w2), full(a2), full(g2p)]
    args += [vecs, mu_l, w2, a2, g2p]
    if has_vres:
        in_specs.append(full(v2p))
        args.append(v2p)
    out_shape = [jax.ShapeDtypeStruct((rows, width), F32)]
    out_specs = [tok(0)]
    if not has_vres:
        out_shape.append(jax.ShapeDtypeStruct((rows, width), F32))
        out_specs.append(tok(0))
    outs = pl.pallas_call(
        functools.partial(_rwkv_kernel, has_vres=has_vres, n_heads=n_heads),
        grid=(batch, nck),
        in_specs=in_specs,
        out_specs=out_specs,
        out_shape=out_shape,
        scratch_shapes=[pltpu.VMEM((n_heads // 2, 2 * RWKV_HEAD, 2 * RWKV_HEAD), F32),
                        pltpu.VMEM((1, width), F32), pltpu.VMEM((1, width), F32),
                        pltpu.VMEM((1, width), F32), pltpu.VMEM((1, LORA_BLOCK), F32),
                        pltpu.VMEM((11, lc, width), F32), pltpu.VMEM((lc, width), F32),
                        pltpu.VMEM((lc, width), F32), pltpu.VMEM((lc, width), F32)],
        compiler_params=_cparams("parallel", "arbitrary"),
        name="rwkv7",
    )(*args)
    return outs if not has_vres else (outs[0], None)


def _merge_kernel(x_ref, ga_ref, gb_ref, u_ref, ys_ref, yb_ref, d_ref, wglu_ref, wua_ref, wub_ref,
                  wo_ref, o_ref):
    y = ys_ref[...] + d_ref[...] * u_ref[...]
    y = jax.nn.gelu(y, approximate=True)
    ya = y * jax.nn.sigmoid(jnp.dot(y.astype(BF16), wglu_ref[...], preferred_element_type=F32))
    up_a = jnp.dot(ya.astype(BF16), wua_ref[...], preferred_element_type=F32)
    up_b = jnp.dot(yb_ref[...].astype(BF16), wub_ref[...], preferred_element_type=F32)
    merged = jax.nn.sigmoid(ga_ref[...]) * up_a + jax.nn.sigmoid(gb_ref[...]) * up_b
    o_ref[...] = x_ref[...] + jnp.dot(merged.astype(BF16), wo_ref[...], preferred_element_type=F32)


def _merge(x, p, ys, yb, d_skip, wglu, wua, wub, wo, *, u_col):
    rows, d = x.shape
    w = ys.shape[1]
    tm = min(256, _row_tile(rows))
    row = lambda width, j: pl.BlockSpec((tm, width), lambda i: (i, j))
    const = lambda a: pl.BlockSpec(a.shape, lambda i: (0, 0), pipeline_mode=pl.Buffered(1))
    return pl.pallas_call(
        _merge_kernel,
        grid=(rows // tm,),
        in_specs=[row(d, 0), row(d, 0), row(d, 1), row(w, u_col // w), row(w, 0), row(w, 0),
                  const(d_skip), const(wglu), const(wua), const(wub), const(wo)],
        out_specs=row(d, 0),
        out_shape=jax.ShapeDtypeStruct((rows, d), F32),
        compiler_params=_cparams("parallel"),
        name="merge",
    )(x, p, p, p, ys, yb, d_skip, wglu, wua, wub, wo)


def _norm_kernel(x_ref, g_ref, o_ref):
    o_ref[...] = _rms(x_ref[...], g_ref[...])


def _final_norm(x, g):
    rows, d = x.shape
    tm = _row_tile(rows)
    return pl.pallas_call(
        _norm_kernel,
        grid=(rows // tm,),
        in_specs=[pl.BlockSpec((tm, d), lambda i: (i, 0)), pl.BlockSpec((1, d), lambda i: (0, 0))],
        out_specs=pl.BlockSpec((tm, d), lambda i: (i, 0)),
        out_shape=jax.ShapeDtypeStruct((rows, d), F32),
        compiler_params=_cparams("parallel"),
        name="final_norm",
    )(x, g)


def _pad_cols(a, width):
    return jnp.pad(a, ((0, 0), (0, width - a.shape[1])))


def _pad_rows(a, height):
    return jnp.pad(a, ((0, height - a.shape[0]), (0, 0)))


def kernel(x, meta_tokens, ffn1_norm, ffn1_w_gate, ffn1_w_up, ffn1_w_down, mix_norm, w_in_first, w_in_rest, mu_shift, mu_vres, ssm_lambda_re, ssm_lambda_im, ssm_log_dt, ssm_b_re, ssm_b_im, ssm_c_re, ssm_c_im, ssm_d, ssm_w_glu, rwkv_w0, rwkv_w2, rwkv_a0, rwkv_a2, rwkv_v0, rwkv_v2, rwkv_g2, rwkv_k_k, rwkv_k_a, rwkv_r_k, rwkv_lnx_w, rwkv_lnx_b, w_up_ssm, w_up_rwkv, w_out, ffn2_norm, ffn2_w_gate, ffn2_w_up, ffn2_w_down, final_norm):
    bsz, seq, d = x.shape
    n_meta = meta_tokens.shape[0]
    depth = ffn1_norm.shape[0]
    w = ssm_d.shape[1]
    n_groups = ssm_b_re.shape[1]
    ld, la, lg, lv = rwkv_w2.shape[1], rwkv_a2.shape[1], rwkv_g2.shape[1], rwkv_v2.shape[1]
    assert w == rwkv_w0.shape[1] and 2 * w == d and w % 128 == 0
    assert n_groups * SSM_GROUP == w and ld + la <= 128 and lg <= 256 and lv <= 128
    t_len = n_meta + seq
    tp = -(-t_len // SEQ_ALIGN) * SEQ_ALIGN
    rows = bsz * tp

    meta = jnp.broadcast_to(meta_tokens[None].astype(x.dtype), (bsz, n_meta, d))
    h_res = jnp.concatenate([meta, x, jnp.zeros((bsz, tp - t_len, d), x.dtype)], axis=1)
    h_res = h_res.reshape(rows, d)

    u_col = 2 * d
    rkv_col = u_col + w
    lora_col = rkv_col + 3 * w
    c_w = 4 * w
    c_g = c_w + ld + la
    c_ga = c_g + lg
    c_gb = c_ga + d
    p_common = c_gb + d

    v_first = None
    for i in range(depth):
        h_res = _ffn(h_res, ffn1_norm[i][None], ffn1_w_gate[i].astype(BF16),
                     ffn1_w_up[i].astype(BF16), ffn1_w_down[i].astype(BF16))

        w_in = w_in_first if i == 0 else w_in_rest[i - 1]
        blocks = [w_in[:, c_ga:c_gb], w_in[:, c_gb:p_common], w_in[:, :c_w],
                  _pad_cols(w_in[:, c_w:c_g], 128), _pad_cols(w_in[:, c_g:c_ga], 256),
                  _pad_cols(w_in[:, p_common:], 128)]
        p, u = _proj(h_res, mix_norm[i][None], jnp.concatenate(blocks, axis=1).astype(BF16),
                     u_col, w)

        ys = _s5_scan(u, ssm_log_dt[i], ssm_lambda_re[i], ssm_lambda_im[i],
                      ssm_c_re[i], ssm_c_im[i], ssm_b_re[i], ssm_b_im[i], batch=bsz)

        mu = mu_shift[i]
        zero_w = jnp.zeros((w,), F32)
        vecs = jnp.stack([rwkv_w0[i], rwkv_a0[i], rwkv_v0[i - 1] if i else zero_w, rwkv_k_k[i],
                          rwkv_k_a[i], rwkv_r_k[i].reshape(w), rwkv_lnx_w[i], rwkv_lnx_b[i],
                          mu[0:w], mu[w:2 * w], mu[2 * w:3 * w]])
        vecs = _pad_rows(vecs, _VEC_ROWS)
        mu_v = mu_vres[i - 1] if i else jnp.zeros((lv,), F32)
        mu_l = jnp.concatenate([jnp.pad(mu[3 * w:3 * w + ld + la], (0, 128 - ld - la)),
                                jnp.pad(mu[3 * w + ld + la:], (0, 256 - lg)),
                                jnp.pad(mu_v, (0, 128 - lv))])[None]
        w2p = _pad_rows(rwkv_w2[i], 128).astype(BF16)
        a2p = jnp.pad(rwkv_a2[i], ((ld, 128 - ld - la), (0, 0))).astype(BF16)
        g2p = _pad_rows(rwkv_g2[i], 256).astype(BF16)
        v2p = _pad_rows(rwkv_v2[i - 1], 128).astype(BF16) if i else None
        yb, v_new = _rwkv(p, v_first, vecs, mu_l, w2p, a2p, g2p, v2p,
                          batch=bsz, width=w, col0=rkv_col, lora_col=lora_col)
        if i == 0:
            v_first = v_new

        h_res = _merge(h_res, p, ys, yb, ssm_d[i][None], ssm_w_glu[i].astype(BF16),
                       w_up_ssm[i].astype(BF16), w_up_rwkv[i].astype(BF16), w_out[i].astype(BF16),
                       u_col=u_col)

        h_res = _ffn(h_res, ffn2_norm[i][None], ffn2_w_gate[i].astype(BF16),
                     ffn2_w_up[i].astype(BF16), ffn2_w_down[i].astype(BF16))

    out = _final_norm(h_res, final_norm[None]).reshape(bsz, tp, d)
    return out[:, n_meta:t_len]
```

```python
import functools
import math

import jax
import jax.numpy as jnp
from jax import lax
from jax.experimental import pallas as pl
from jax.experimental.pallas import tpu as pltpu

F32 = jnp.float32
BF16 = jnp.bfloat16
HIGHEST = lax.Precision.HIGHEST

NORM_EPS = 1e-6
RWKV_HEAD = 64
LNX_EPS = RWKV_HEAD * 1e-5
SSM_GROUP = 16
S5_CHUNK = 16
RWKV_CHUNK = 64
RWKV_SUBCHUNKS = 3
RWKV_GROUPS = 1
PROJ_ROW_TILE = 1536
LORA_BLOCK = 512
SEQ_ALIGN = 128
VMEM_LIMIT_BYTES = 56 * 1024 * 1024


def _cparams(*sem):
    return pltpu.CompilerParams(dimension_semantics=sem, vmem_limit_bytes=VMEM_LIMIT_BYTES)


def _rms(x, g):
    inv = lax.rsqrt(jnp.mean(x * x, axis=-1, keepdims=True) + NORM_EPS)
    return (x * inv) * g


def _row_tile(rows):
    for tm in (512, 256, 128, 64, 32, 16, 8):
        if rows % tm == 0:
            return tm
    raise ValueError(rows)


def _col_tile(cols):
    for tn in (512, 256, 128):
        if cols % tn == 0:
            return tn
    raise ValueError(cols)


def _ffn_kernel(x_ref, g_ref, wg_ref, wu_ref, wd_ref, o_ref, h_sc):
    f = pl.program_id(1)

    @pl.when(f == 0)
    def _():
        x = x_ref[...]
        h_sc[...] = _rms(x, g_ref[...]).astype(BF16)
        o_ref[...] = x

    h = h_sc[...]
    a = jnp.dot(h, wg_ref[...], preferred_element_type=F32)
    b = jnp.dot(h, wu_ref[...], preferred_element_type=F32)
    z = (0.5 * (a * jax.nn.sigmoid(a)) * b).astype(BF16)
    o_ref[...] += jnp.dot(z, wd_ref[...], preferred_element_type=F32)


def _ffn(x, g, wg, wu, wd, layer):
    rows, d = x.shape
    dff = wg.shape[2]
    tm = _row_tile(rows)
    tf = _col_tile(dff)
    return pl.pallas_call(
        _ffn_kernel,
        grid=(rows // tm, dff // tf),
        in_specs=[
            pl.BlockSpec((tm, d), lambda i, f: (i, 0)),
            pl.BlockSpec((1, d), lambda i, f: (0, 0)),
            pl.BlockSpec((None, d, tf), lambda i, f: (layer, 0, f)),
            pl.BlockSpec((None, d, tf), lambda i, f: (layer, 0, f)),
            pl.BlockSpec((None, tf, d), lambda i, f: (layer, f, 0)),
        ],
        out_specs=pl.BlockSpec((tm, d), lambda i, f: (i, 0)),
        out_shape=jax.ShapeDtypeStruct((rows, d), F32),
        scratch_shapes=[pltpu.VMEM((tm, d), BF16)],
        compiler_params=_cparams("parallel", "arbitrary"),
        name="ffn",
    )(x, g, wg, wu, wd)


def _proj_kernel(x_ref, g_ref, w_ref, o_ref, h_sc):
    @pl.when(pl.program_id(1) == 0)
    def _():
        h_sc[...] = _rms(x_ref[...], g_ref[...]).astype(BF16)

    o_ref[...] = jnp.dot(h_sc[...], w_ref[...], preferred_element_type=F32)


def _proj_split_kernel(x_ref, g_ref, w_ref, o_ref, u_ref, h_sc, *, j0, nu):
    _proj_kernel(x_ref, g_ref, w_ref, o_ref, h_sc)
    j = pl.program_id(1)

    @pl.when(jnp.logical_and(j >= j0, j < j0 + nu))
    def _():
        u_ref[...] = o_ref[...]


def _proj(x, g, w, u_col, u_width):
    rows, d = x.shape
    n = w.shape[1]
    tm = PROJ_ROW_TILE if rows % PROJ_ROW_TILE == 0 else _row_tile(rows)
    tn = _col_tile(n)
    split = u_col % tn == 0 and u_width % tn == 0
    in_specs = [
        pl.BlockSpec((tm, d), lambda i, j: (i, 0)),
        pl.BlockSpec((1, d), lambda i, j: (0, 0)),
        pl.BlockSpec((d, tn), lambda i, j: (0, j)),
    ]
    p_spec = pl.BlockSpec((tm, tn), lambda i, j: (i, j))
    p_shape = jax.ShapeDtypeStruct((rows, n), F32)
    common = dict(grid=(rows // tm, n // tn), in_specs=in_specs,
                  scratch_shapes=[pltpu.VMEM((tm, d), BF16)],
                  compiler_params=_cparams("parallel", "arbitrary"), name="in_proj")
    if not split:
        p = pl.pallas_call(_proj_kernel, out_specs=p_spec, out_shape=p_shape, **common)(x, g, w)
        return p, p[:, u_col:u_col + u_width]
    j0, nu = u_col // tn, u_width // tn
    u_spec = pl.BlockSpec((tm, tn), lambda i, j: (i, jnp.clip(j - j0, 0, nu - 1)))
    return pl.pallas_call(
        functools.partial(_proj_split_kernel, j0=j0, nu=nu),
        out_specs=[p_spec, u_spec],
        out_shape=[p_shape, jax.ShapeDtypeStruct((rows, u_width), F32)],
        **common)(x, g, w)


def _cpow(lr, li, dt, m):
    mag = jnp.exp(lr * dt * m)
    ang = li * dt * m
    return mag * jnp.cos(ang), mag * jnp.sin(ang)


def _s5_param_kernel(ldt_ref, lrc_ref, lic_ref, lrr_ref, lir_ref, ctre_ref, ctim_ref,
                     btre_ref, btim_ref, bttre_ref, bttim_ref, m0_ref, e_ref, p_ref, ap_ref,
                     *, n_chunks):
    lc = S5_CHUNK
    cw = SSM_GROUP
    width = lc * cw
    dt = jnp.exp(ldt_ref[...])
    lr_c, li_c = lrc_ref[...], lic_ref[...]
    lr_r, li_r = lrr_ref[...], lir_ref[...]

    are, aim = _cpow(lr_r, li_r, dt, 1.0)
    den = lr_r * lr_r + li_r * li_r
    nr, ni = are - 1.0, aim
    q_re = (nr * lr_r + ni * li_r) / den
    q_im = (ni * lr_r - nr * li_r) / den

    lag = (lax.broadcasted_iota(jnp.int32, (1, width), 1) // cw).astype(F32)
    ct_re, ct_im = ctre_ref[...], ctim_ref[...]
    p_re, p_im = _cpow(lr_c, li_c, dt, lag)
    d0_re = ct_re * p_re - ct_im * p_im
    d0_im = ct_re * p_im + ct_im * p_re
    bt_re, bt_im = btre_ref[...], btim_ref[...]
    bbt_re = q_re * bt_re - q_im * bt_im
    bbt_im = q_re * bt_im + q_im * bt_re
    m0_ref[...] = (jnp.dot(bbt_re, d0_re, preferred_element_type=F32, precision=HIGHEST)
                   - jnp.dot(bbt_im, d0_im, preferred_element_type=F32, precision=HIGHEST))

    rowj = (lax.broadcasted_iota(jnp.int32, (width, 1), 0) // cw).astype(F32)
    e_pre, e_pim = _cpow(lr_r, li_r, dt, (lc - 1.0) - rowj)
    btt_re, btt_im = bttre_ref[...], bttim_ref[...]
    bb2_re = q_re * btt_re - q_im * btt_im
    bb2_im = q_re * btt_im + q_im * btt_re
    e_ref[...] = jnp.concatenate([bb2_re * e_pre - bb2_im * e_pim,
                                  bb2_re * e_pim + bb2_im * e_pre], axis=1)

    p1_re, p1_im = _cpow(lr_c, li_c, dt, lag + 1.0)
    p_ref[...] = jnp.concatenate([ct_re * p1_re - ct_im * p1_im,
                                  -(ct_re * p1_im + ct_im * p1_re)], axis=0)

    ap_ref[...] = jnp.zeros_like(ap_ref)
    for i, shift in enumerate(_scan_shifts(n_chunks)):
        pre, pim = _cpow(lr_r, li_r, dt, float(lc * shift))
        ap_ref[2 * i:2 * i + 1, :] = jnp.concatenate([pre, pre], axis=1)
        ap_ref[2 * i + 1:2 * i + 2, :] = jnp.concatenate([-pim, pim], axis=1)


def _scan_shifts(n_chunks):
    shifts, s = [], 1
    while s < n_chunks:
        shifts.append(s)
        s *= 2
    return shifts


def _s5_main_kernel(*refs, n_chunks, n_state):
    lc = S5_CHUNK
    cw = SSM_GROUP
    x_refs = refs[:lc]
    m0_ref, e_ref, p_ref, ap_ref, y_ref, m_sc, e_sc, p_sc, ap_sc = refs[lc:]
    lb = x_refs[0].shape[1]
    gpb = lb // cw
    width = lc * cw
    sw = 2 * n_state

    @pl.when(pl.program_id(1) == 0)
    def _():
        def iota(shape, dim):
            return lax.broadcasted_iota(jnp.int32, shape, dim)

        def spread(shape, dim_small):
            small, big = iota(shape, dim_small), iota(shape, 1 - dim_small)
            return jnp.logical_and(small // cw == big // lb, small % cw == big % cw)

        lane_expand = spread((width, lc * lb), 0).astype(BF16)
        row_expand = spread((lc * lb, width), 1).astype(BF16)

        m0w = jnp.dot(m0_ref[...].astype(BF16), lane_expand, preferred_element_type=F32)
        own = iota(m0w.shape, 0) // cw == (iota(m0w.shape, 1) % lb) // cw
        m0 = jnp.where(own, m0w, 0.0).astype(BF16)
        for t in range(lc):
            blk = m0 if t == 0 else jnp.concatenate(
                [jnp.zeros((lb, t * lb), m0.dtype), m0[:, :(lc - t) * lb]], axis=1)
            m_sc[t * lb:(t + 1) * lb, :] = blk

        e_stack = jnp.concatenate([e_ref[g] for g in range(gpb)], axis=1).astype(BF16)
        e_full = jnp.dot(row_expand, e_stack, preferred_element_type=F32)
        own = (iota(e_full.shape, 0) % lb) // cw == iota(e_full.shape, 1) // sw
        e_sc[...] = jnp.where(own, e_full, 0.0).astype(BF16)

        p_stack = p_ref[...].reshape(gpb * sw, width).astype(BF16)
        p_full = jnp.dot(p_stack, lane_expand, preferred_element_type=F32)
        own = iota(p_full.shape, 0) // sw == (iota(p_full.shape, 1) % lb) // cw
        p_sc[...] = jnp.where(own, p_full, 0.0).astype(BF16)

        ap_sc[...] = jnp.concatenate([ap_ref[g] for g in range(gpb)], axis=1)

    x = jnp.concatenate([r[...].astype(BF16) for r in x_refs], axis=1)
    y = jnp.concatenate(
        [jnp.dot(x[:, :(q + 2) * lb], m_sc[:(q + 2) * lb, q * lb:(q + 2) * lb],
                 preferred_element_type=F32) for q in range(0, lc, 2)], axis=1)
    s = jnp.dot(x, e_sc[...], preferred_element_type=F32)

    def swap_re_im(a):
        return jnp.concatenate([pltpu.roll(a[:, j * sw:(j + 1) * sw], n_state, axis=1)
                                for j in range(a.shape[1] // sw)], axis=1)

    crow = lax.broadcasted_iota(jnp.int32, (n_chunks, 1), 0)
    for i, shift in enumerate(_scan_shifts(n_chunks)):
        prev = jnp.where(crow >= shift, pltpu.roll(s, shift, axis=0), 0.0)
        s = s + ap_sc[2 * i:2 * i + 1, :] * prev + ap_sc[2 * i + 1:2 * i + 2, :] * swap_re_im(prev)
    s_start = jnp.where(crow >= 1, pltpu.roll(s, 1, axis=0), 0.0)
    y = y + jnp.dot(s_start.astype(BF16), p_sc[...], preferred_element_type=F32)
    for t in range(lc):
        y_ref[t] = y[:, t * lb:(t + 1) * lb]


def _s5_scan(u, ldt, lr, li, c_re, c_im, b_re, b_im, *, batch):
    rows, w = u.shape
    g, n = lr.shape
    cw = SSM_GROUP
    lc = S5_CHUNK
    width = lc * cw
    lb = 128
    gpb = lb // cw
    n_oct = w // lb
    n_chunks = rows // batch // lc
    ap_rows = -(-2 * len(_scan_shifts(n_chunks)) // 8) * 8
    ct_re = jnp.tile(jnp.swapaxes(c_re, 1, 2), (1, 1, lc))
    ct_im = jnp.tile(jnp.swapaxes(c_im, 1, 2), (1, 1, lc))
    bt_re = jnp.swapaxes(b_re, 1, 2)
    bt_im = jnp.swapaxes(b_im, 1, 2)
    btt_re = jnp.tile(bt_re, (1, lc, 1))
    btt_im = jnp.tile(bt_im, (1, lc, 1))
    g3 = lambda a, b: pl.BlockSpec((None, a, b), lambda i: (i, 0, 0))
    m0, e_g, p_g, ap = pl.pallas_call(
        functools.partial(_s5_param_kernel, n_chunks=n_chunks),
        grid=(g,),
        in_specs=[g3(1, 1), g3(n, 1), g3(n, 1), g3(1, n), g3(1, n),
                  g3(n, width), g3(n, width), g3(cw, n), g3(cw, n), g3(width, n), g3(width, n)],
        out_specs=[g3(cw, width), g3(width, 2 * n), g3(2 * n, width), g3(ap_rows, 2 * n)],
        out_shape=[jax.ShapeDtypeStruct((g, cw, width), F32),
                   jax.ShapeDtypeStruct((g, width, 2 * n), F32),
                   jax.ShapeDtypeStruct((g, 2 * n, width), F32),
                   jax.ShapeDtypeStruct((g, ap_rows, 2 * n), F32)],
        compiler_params=_cparams("parallel"),
        name="s5_params",
    )(ldt.reshape(g, 1, 1), lr.reshape(g, n, 1), li.reshape(g, n, 1), lr.reshape(g, 1, n),
      li.reshape(g, 1, n), ct_re, ct_im, bt_re, bt_im, btt_re, btt_im)

    m0_o = m0.reshape(n_oct, lb, width)
    e_o = e_g.reshape(n_oct, gpb, width, 2 * n)
    p_o = p_g.reshape(n_oct, gpb, 2 * n, width)
    ap_o = ap.reshape(n_oct, gpb, ap_rows, 2 * n)

    u2 = u.reshape(rows // lc, lc * w)
    xspec = lambda t: pl.BlockSpec((n_chunks, lb), lambda o, b: (b, t * n_oct + o))
    per_oct = lambda a: pl.BlockSpec((None,) + a.shape[1:], lambda o, b: (o,) + (0,) * (a.ndim - 1))
    ys = pl.pallas_call(
        functools.partial(_s5_main_kernel, n_chunks=n_chunks, n_state=n),
        grid=(n_oct, batch),
        in_specs=[xspec(t) for t in range(lc)] + [per_oct(m0_o), per_oct(e_o), per_oct(p_o),
                                                   per_oct(ap_o)],
        out_specs=pl.BlockSpec((lc, n_chunks, lb), lambda o, b: (0, b, o)),
        out_shape=jax.ShapeDtypeStruct((lc, rows // lc, w), F32),
        scratch_shapes=[pltpu.VMEM((lc * lb, lc * lb), BF16),
                        pltpu.VMEM((lc * lb, gpb * 2 * n), BF16),
                        pltpu.VMEM((gpb * 2 * n, lc * lb), BF16),
                        pltpu.VMEM((ap_rows, gpb * 2 * n), F32)],
        compiler_params=_cparams("parallel", "arbitrary"),
        name="s5_scan",
    )(*([u2] * lc), m0_o, e_o, p_o, ap_o)
    return jnp.swapaxes(ys, 0, 1).reshape(rows, w)


(_V_W0, _V_A0, _V_V0, _V_KK, _V_KA, _V_RK, _V_LNW, _V_LNB, _V_MUR, _V_MUK, _V_MUV) = range(11)
_VEC_ROWS = 16


def _dot_nt(a, b, **kw):
    return lax.dot_general(a, b, (((1,), (1,)), ((), ())), preferred_element_type=F32, **kw)


def _dot_tn(a, b, **kw):
    return lax.dot_general(a, b, (((0,), (0,)), ((), ())), preferred_element_type=F32, **kw)


def _mm(a, b):
    return jnp.dot(a.astype(BF16), b.astype(BF16), preferred_element_type=F32)


def _rwkv_solve(pairs):
    lc, pw = pairs[0][0].shape
    hd = RWKV_HEAD
    assert lc == hd and pw == 2 * hd
    lane = lax.broadcasted_iota(jnp.int32, (lc, pw), 1)
    row = lax.broadcasted_iota(jnp.int32, (lc, pw), 0)
    first = lane < hd
    col = jnp.where(first, lane, lane - hd)
    strict2 = row > col
    incl2 = row >= col
    ak_mask = jnp.logical_and(strict2, jnp.logical_not(first))
    zeros = jnp.zeros((lc, pw), F32)
    heads = [(p, e) for p in range(len(pairs)) for e in range(2)]

    gms = []
    for rt, at, bt, kt, v in pairs:
        ar = jnp.concatenate([jnp.where(first, at, 0.0), jnp.where(first, rt, 0.0),
                              jnp.where(first, 0.0, at), jnp.where(first, 0.0, rt)], axis=0)
        gms.append(_dot_nt(ar.astype(BF16), jnp.concatenate([bt, kt], axis=0).astype(BF16)))
    ga = [gms[p][2 * e * lc:(2 * e + 1) * lc] for p, e in heads]
    gr = [gms[p][(2 * e + 1) * lc:(2 * e + 2) * lc] for p, e in heads]
    vr = [pltpu.roll(pr[4], hd, axis=1) for pr in pairs]
    vz = [jnp.concatenate([zeros, r], axis=0).astype(BF16) for r in vr]

    nk = [jnp.where(strict2[:, :lc], g[:, :lc], 0.0).astype(BF16) for g in ga]
    y = []
    for i, (p, e) in enumerate(heads):
        akv = _mm(jnp.where(ak_mask, ga[i], 0.0), vz[p])
        y.append(jnp.where(first, pairs[p][1], akv) if e == 0 else jnp.where(first, akv, pairs[p][1]))
    steps = max(1, (lc - 1).bit_length())
    for i in range(steps):
        y = [yy + _mm(nn, yy) for nn, yy in zip(nk, y)]
        if i + 1 < steps:
            nk = [_mm(nn, nn).astype(BF16) for nn in nk]
    rz = []
    for i, (p, e) in enumerate(heads):
        zv = jnp.where(first, 0.0, vr[p]) if e == 0 else jnp.where(first, vr[p], 0.0)
        rz.append(_mm(jnp.where(incl2, gr[i], 0.0), jnp.concatenate([y[i], zv], axis=0)))

    outs = []
    for p, (rt, at, bt, kt, v) in enumerate(pairs):
        y0, y1, rz0, rz1 = y[2 * p], y[2 * p + 1], rz[2 * p], rz[2 * p + 1]
        outs.append((jnp.where(first, y0, y1),
                     rt + jnp.where(first, rz0, rz1),
                     pltpu.roll(jnp.where(first, y1, y0), hd, axis=1),
                     pltpu.roll(jnp.where(first, rz1, rz0), hd, axis=1)))
    return outs


def _rwkv_state(solved, tails, states):
    lc, pw = solved[0][0].shape
    hd = RWKV_HEAD
    uos = [_dot_nt(jnp.concatenate([x1, z1], axis=0).astype(BF16), s0.astype(BF16))
           for (x1, z1, x2, z2), s0 in zip(solved, states)]
    si = lax.broadcasted_iota(jnp.int32, (pw, pw), 0)
    sj = lax.broadcasted_iota(jnp.int32, (pw, pw), 1)
    same_head = (si < hd) == (sj < hd)
    outs = []
    for uo, (x1, z1, x2, z2), (v, bh, kh, wl), s0 in zip(uos, solved, tails, states):
        u = uo[:lc] + x2
        o = uo[lc:] + z2
        upd = _dot_tn(jnp.concatenate([u, v], axis=0).astype(BF16),
                      jnp.concatenate([bh, kh], axis=0).astype(BF16))
        outs.append((o, s0 * wl + jnp.where(same_head, upd, 0.0)))
    return outs


def _rwkv_kernel(*refs, has_vres, n_heads):
    if has_vres:
        (r_ref, k_ref, v_ref, l_ref, vf_ref, vec_ref, mul_ref, w2_ref, a2_ref, g2_ref, v2_ref,
         o_ref, s_sc, pr_sc, pk_sc, pv_sc, pl_sc, big_sc, on_sc, bonus_sc, gate_sc) = refs
        vout_ref = None
    else:
        (r_ref, k_ref, v_ref, l_ref, vec_ref, mul_ref, w2_ref, a2_ref, g2_ref,
         o_ref, vout_ref, s_sc, pr_sc, pk_sc, pv_sc, pl_sc, big_sc, on_sc, bonus_sc, gate_sc) = refs
    nr, width = r_ref.shape
    lc = RWKV_CHUNK
    n_sub = nr // lc
    hd = RWKV_HEAD
    pw = 2 * hd

    @pl.when(pl.program_id(1) == 0)
    def _():
        s_sc[...] = jnp.zeros_like(s_sc)
        pr_sc[...] = jnp.zeros_like(pr_sc)
        pk_sc[...] = jnp.zeros_like(pk_sc)
        pv_sc[...] = jnp.zeros_like(pv_sc)
        pl_sc[...] = jnp.zeros_like(pl_sc)

    def vec(i):
        return vec_ref[i:i + 1, :]

    n_grp = RWKV_GROUPS if n_sub % RWKV_GROUPS == 0 else 1
    cpg = n_sub // n_grp
    g_rows = cpg * lc

    def prologue(g, prevs):
        rows = slice(g * g_rows, (g + 1) * g_rows)
        row = lax.broadcasted_iota(jnp.int32, (g_rows, 1), 0)

        def shift_mix(x_ref, prev, mu):
            x = x_ref[rows, :]
            before = jnp.where(row == 0, prev, pltpu.roll(x, 1, axis=0))
            return x + mu * (before - x), x[g_rows - 1:g_rows, :]

        xr, last_r = shift_mix(r_ref, prevs[0], vec(_V_MUR))
        xk, last_k = shift_mix(k_ref, prevs[1], vec(_V_MUK))
        xv, last_v = shift_mix(v_ref, prevs[2], vec(_V_MUV))
        xl, last_l = shift_mix(l_ref, prevs[3], mul_ref[...])
        xwa, xg, xvr = xl[:, 0:128], xl[:, 128:384], xl[:, 384:512]

        logw = (-math.exp(-0.5)) * jax.nn.sigmoid(vec(_V_W0) + _mm(jnp.tanh(xwa), w2_ref[...]))
        iclr = jax.nn.sigmoid(vec(_V_A0) + _mm(xwa, a2_ref[...]))
        gate_sc[rows, :] = _mm(jax.nn.sigmoid(xg), g2_ref[...])
        if has_vres:
            v = xv + (vf_ref[rows, :] - xv) * jax.nn.sigmoid(vec(_V_V0) + _mm(xvr, v2_ref[...]))
        else:
            v = xv
            vout_ref[rows, :] = xv
        k = xk * (1.0 + (iclr - 1.0) * vec(_V_KA))

        ti = lax.broadcasted_iota(jnp.int32, (g_rows, g_rows), 0)
        tj = lax.broadcasted_iota(jnp.int32, (g_rows, g_rows), 1)
        tri = jnp.logical_and(ti >= tj, ti // lc == tj // lc)
        cum = jnp.dot(tri.astype(F32), logw, preferred_element_type=F32, precision=HIGHEST)
        cum_last = cum[lc - 1:lc, :]
        for h in range(1, cpg):
            cum_last = jnp.where(row // lc == h, cum[(h + 1) * lc - 1:(h + 1) * lc, :], cum_last)
        dec_inv = jnp.exp(-cum)
        dec_tail = jnp.exp(cum_last - cum)

        big_sc[0, rows, :] = xr * jnp.exp(cum)
        big_sc[1, rows, :] = xk * vec(_V_KK)
        big_sc[2, rows, :] = iclr
        big_sc[3, rows, :] = k * dec_inv
        big_sc[4, rows, :] = v
        big_sc[5, rows, :] = jnp.exp(cum - logw)
        big_sc[6, rows, :] = k * dec_tail
        big_sc[7, rows, :] = xr * k * vec(_V_RK)
        big_sc[8, rows, :] = dec_inv
        big_sc[9, rows, :] = dec_tail
        big_sc[10, rows, :] = jnp.broadcast_to(jnp.exp(cum_last), (g_rows, width))
        return last_r, last_k, last_v, last_l

    prevs = (pr_sc[...], pk_sc[...], pv_sc[...], pl_sc[...])
    for g in range(n_grp):
        prevs = prologue(g, prevs)
    pr_sc[...], pk_sc[...], pv_sc[...], pl_sc[...] = prevs

    first = lax.broadcasted_iota(jnp.int32, (lc, pw), 1) < hd

    def head_sum(x):
        s0 = jnp.sum(jnp.where(first, x, 0.0), axis=-1, keepdims=True)
        s1 = jnp.sum(jnp.where(first, 0.0, x), axis=-1, keepdims=True)
        return jnp.where(first, s0, s1)

    n_pairs = n_heads // 2
    states = [s_sc[hp] for hp in range(n_pairs)]
    for g in range(n_grp):
        chunks = range(g * cpg, (g + 1) * cpg)
        ins, tails = [], []
        for h in chunks:
            for hp in range(n_pairs):
                rows, lanes = slice(h * lc, (h + 1) * lc), slice(hp * pw, (hp + 1) * pw)
                rt, kkraw, iclr_p, kt, v_p, dex, kh, rk, dinv, dtail, wl = [
                    big_sc[i, rows, lanes] for i in range(11)]
                kkn = kkraw * lax.rsqrt(head_sum(kkraw * kkraw) + 1e-12)
                b = kkn * iclr_p
                ins.append((rt, -kkn * dex, b * dinv, kt, v_p))
                tails.append((v_p, b * dtail, kh, wl[0:1]))
                bonus_sc[rows, lanes] = head_sum(rk) * v_p
        solved = _rwkv_solve(ins)
        for j, h in enumerate(chunks):
            sel = slice(j * n_pairs, (j + 1) * n_pairs)
            outs = _rwkv_state(solved[sel], tails[sel], states)
            states = [s_new for _, s_new in outs]
            for hp, (o, _) in enumerate(outs):
                cen = o - head_sum(o) * (1.0 / hd)
                var = head_sum(cen * cen) * (1.0 / hd)
                on_sc[h * lc:(h + 1) * lc, hp * pw:(hp + 1) * pw] = cen * lax.rsqrt(var + LNX_EPS)
    for hp in range(n_pairs):
        s_sc[hp] = states[hp]
    o_ref[...] = (on_sc[...] * vec(_V_LNW) + vec(_V_LNB) + bonus_sc[...]) * gate_sc[...]


def _rwkv(p, vfirst, vecs, mu_l, w2, a2, g2p, v2p, *, batch, width, col0, lora_col):
    rows = p.shape[0]
    tp = rows // batch
    n_sub = RWKV_SUBCHUNKS if tp % (RWKV_SUBCHUNKS * RWKV_CHUNK) == 0 else 1
    lc = n_sub * RWKV_CHUNK
    nck = tp // lc
    n_heads = width // RWKV_HEAD
    has_vres = vfirst is not None
    rowblk = lambda b, c: b * nck + c
    cb = col0 // width
    tok = lambda j: pl.BlockSpec((lc, width), lambda b, c: (rowblk(b, c), j))
    full = lambda a: pl.BlockSpec(a.shape, lambda b, c: (0,) * a.ndim)
    in_specs = [tok(cb), tok(cb + 1), tok(cb + 2),
                pl.BlockSpec((lc, LORA_BLOCK), lambda b, c: (rowblk(b, c), lora_col // LORA_BLOCK))]
    args = [p, p, p, p]
    if has_vres:
        in_specs.append(tok(0))
        args.append(vfirst)
    in_specs += [full(vecs), full(mu_l), full(w2), full(a2), full(g2p)]
    args += [vecs, mu_l, w2, a2, g2p]
    if has_vres:
        in_specs.append(full(v2p))
        args.append(v2p)
    out_shape = [jax.ShapeDtypeStruct((rows, width), F32)]
    out_specs = [tok(0)]
    if not has_vres:
        out_shape.append(jax.ShapeDtypeStruct((rows, width), F32))
        out_specs.append(tok(0))
    outs = pl.pallas_call(
        functools.partial(_rwkv_kernel, has_vres=has_vres, n_heads=n_heads),
        grid=(batch, nck),
        in_specs=in_specs,
        out_specs=out_specs,
        out_shape=out_shape,
        scratch_shapes=[pltpu.VMEM((n_heads // 2, 2 * RWKV_HEAD, 2 * RWKV_HEAD), F32),
                        pltpu.VMEM((1, width), F32), pltpu.VMEM((1, width), F32),
                        pltpu.VMEM((1, width), F32), pltpu.VMEM((1, LORA_BLOCK), F32),
                        pltpu.VMEM((11, lc, width), F32), pltpu.VMEM((lc, width), F32),
                        pltpu.VMEM((lc, width), F32), pltpu.VMEM((lc, width), F32)],
        compiler_params=_cparams("parallel", "arbitrary"),
        name="rwkv7",
    )(*args)
    return outs if not has_vres else (outs[0], None)


def _merge_kernel(x_ref, ga_ref, gb_ref, u_ref, ys_ref, yb_ref, d_ref, wglu_ref, wua_ref, wub_ref,
                  wo_ref, o_ref):
    y = ys_ref[...] + d_ref[...] * u_ref[...]
    y = jax.nn.gelu(y, approximate=True)
    ya = y * jax.nn.sigmoid(jnp.dot(y.astype(BF16), wglu_ref[...], preferred_element_type=F32))
    up_a = jnp.dot(ya.astype(BF16), wua_ref[...], preferred_element_type=F32)
    up_b = jnp.dot(yb_ref[...].astype(BF16), wub_ref[...], preferred_element_type=F32)
    merged = jax.nn.sigmoid(ga_ref[...]) * up_a + jax.nn.sigmoid(gb_ref[...]) * up_b
    o_ref[...] = x_ref[...] + jnp.dot(merged.astype(BF16), wo_ref[...], preferred_element_type=F32)


def _merge(x, p, ys, yb, d_skip, wglu, wua, wub, wo, *, u_col, layer):
    rows, d = x.shape
    w = ys.shape[1]
    tm = min(256, _row_tile(rows))
    row = lambda width, j: pl.BlockSpec((tm, width), lambda i: (i, j))
    const = lambda a: pl.BlockSpec((None,) + a.shape[1:], lambda i: (layer, 0, 0),
                                   pipeline_mode=pl.Buffered(1))
    return pl.pallas_call(
        _merge_kernel,
        grid=(rows // tm,),
        in_specs=[row(d, 0), row(d, 0), row(d, 1), row(w, u_col // w), row(w, 0), row(w, 0),
                  pl.BlockSpec(d_skip.shape, lambda i: (0, 0)),
                  const(wglu), const(wua), const(wub), const(wo)],
        out_specs=row(d, 0),
        out_shape=jax.ShapeDtypeStruct((rows, d), F32),
        compiler_params=_cparams("parallel"),
        name="merge",
    )(x, p, p, p, ys, yb, d_skip, wglu, wua, wub, wo)


def _norm_kernel(x_ref, g_ref, o_ref):
    o_ref[...] = _rms(x_ref[...], g_ref[...])


def _final_norm(x, g):
    rows, d = x.shape
    tm = _row_tile(rows)
    return pl.pallas_call(
        _norm_kernel,
        grid=(rows // tm,),
        in_specs=[pl.BlockSpec((tm, d), lambda i: (i, 0)), pl.BlockSpec((1, d), lambda i: (0, 0))],
        out_specs=pl.BlockSpec((tm, d), lambda i: (i, 0)),
        out_shape=jax.ShapeDtypeStruct((rows, d), F32),
        compiler_params=_cparams("parallel"),
        name="final_norm",
    )(x, g)


def _pad_cols(a, width):
    return jnp.pad(a, ((0, 0), (0, width - a.shape[1])))


def _pad_rows(a, height):
    return jnp.pad(a, ((0, height - a.shape[0]), (0, 0)))


def kernel(x, meta_tokens, ffn1_norm, ffn1_w_gate, ffn1_w_up, ffn1_w_down, mix_norm, w_in_first, w_in_rest, mu_shift, mu_vres, ssm_lambda_re, ssm_lambda_im, ssm_log_dt, ssm_b_re, ssm_b_im, ssm_c_re, ssm_c_im, ssm_d, ssm_w_glu, rwkv_w0, rwkv_w2, rwkv_a0, rwkv_a2, rwkv_v0, rwkv_v2, rwkv_g2, rwkv_k_k, rwkv_k_a, rwkv_r_k, rwkv_lnx_w, rwkv_lnx_b, w_up_ssm, w_up_rwkv, w_out, ffn2_norm, ffn2_w_gate, ffn2_w_up, ffn2_w_down, final_norm):
    bsz, seq, d = x.shape
    n_meta = meta_tokens.shape[0]
    depth = ffn1_norm.shape[0]
    w = ssm_d.shape[1]
    n_groups = ssm_b_re.shape[1]
    ld, la, lg, lv = rwkv_w2.shape[1], rwkv_a2.shape[1], rwkv_g2.shape[1], rwkv_v2.shape[1]
    assert w == rwkv_w0.shape[1] and 2 * w == d and w % 128 == 0
    assert n_groups * SSM_GROUP == w and ld + la <= 128 and lg <= 256 and lv <= 128
    t_len = n_meta + seq
    tp = -(-t_len // SEQ_ALIGN) * SEQ_ALIGN
    rows = bsz * tp

    meta = jnp.broadcast_to(meta_tokens[None].astype(x.dtype), (bsz, n_meta, d))
    h_res = jnp.concatenate([meta, x, jnp.zeros((bsz, tp - t_len, d), x.dtype)], axis=1)
    h_res = h_res.reshape(rows, d)

    u_col = 2 * d
    rkv_col = u_col + w
    lora_col = rkv_col + 3 * w
    c_w = 4 * w
    c_g = c_w + ld + la
    c_ga = c_g + lg
    c_gb = c_ga + d
    p_common = c_gb + d

    ffn1_w = [a.astype(BF16) for a in (ffn1_w_gate, ffn1_w_up, ffn1_w_down)]
    ffn2_w = [a.astype(BF16) for a in (ffn2_w_gate, ffn2_w_up, ffn2_w_down)]
    merge_w = [a.astype(BF16) for a in (ssm_w_glu, w_up_ssm, w_up_rwkv, w_out)]

    v_first = None
    for i in range(depth):
        h_res = _ffn(h_res, ffn1_norm[i][None], *ffn1_w, i)

        w_in = w_in_first if i == 0 else w_in_rest[i - 1]
        blocks = [w_in[:, c_ga:c_gb], w_in[:, c_gb:p_common], w_in[:, :c_w],
                  _pad_cols(w_in[:, c_w:c_g], 128), _pad_cols(w_in[:, c_g:c_ga], 256),
                  _pad_cols(w_in[:, p_common:], 128)]
        p, u = _proj(h_res, mix_norm[i][None], jnp.concatenate(blocks, axis=1).astype(BF16),
                     u_col, w)

        ys = _s5_scan(u, ssm_log_dt[i], ssm_lambda_re[i], ssm_lambda_im[i],
                      ssm_c_re[i], ssm_c_im[i], ssm_b_re[i], ssm_b_im[i], batch=bsz)

        mu = mu_shift[i]
        zero_w = jnp.zeros((w,), F32)
        vecs = jnp.stack([rwkv_w0[i], rwkv_a0[i], rwkv_v0[i - 1] if i else zero_w, rwkv_k_k[i],
                          rwkv_k_a[i], rwkv_r_k[i].reshape(w), rwkv_lnx_w[i], rwkv_lnx_b[i],
                          mu[0:w], mu[w:2 * w], mu[2 * w:3 * w]])
        vecs = _pad_rows(vecs, _VEC_ROWS)
        mu_v = mu_vres[i - 1] if i else jnp.zeros((lv,), F32)
        mu_l = jnp.concatenate([jnp.pad(mu[3 * w:3 * w + ld + la], (0, 128 - ld - la)),
                                jnp.pad(mu[3 * w + ld + la:], (0, 256 - lg)),
                                jnp.pad(mu_v, (0, 128 - lv))])[None]
        w2p = _pad_rows(rwkv_w2[i], 128).astype(BF16)
        a2p = jnp.pad(rwkv_a2[i], ((ld, 128 - ld - la), (0, 0))).astype(BF16)
        g2p = _pad_rows(rwkv_g2[i], 256).astype(BF16)
        v2p = _pad_rows(rwkv_v2[i - 1], 128).astype(BF16) if i else None
        yb, v_new = _rwkv(p, v_first, vecs, mu_l, w2p, a2p, g2p, v2p,
                          batch=bsz, width=w, col0=rkv_col, lora_col=lora_col)
        if i == 0:
            v_first = v_new

        h_res = _merge(h_res, p, ys, yb, ssm_d[i][None], *merge_w, u_col=u_col, layer=i)

        h_res = _ffn(h_res, ffn2_norm[i][None], *ffn2_w, i)

    out = _final_norm(h_res, final_norm[None]).reshape(bsz, tp, d)
    return out[:, n_meta:t_len]
```

```python
import functools
import math

import jax
import jax.numpy as jnp
from jax import lax
from jax.experimental import pallas as pl
from jax.experimental.pallas import tpu as pltpu

F32 = jnp.float32
BF16 = jnp.bfloat16
HIGHEST = lax.Precision.HIGHEST

NORM_EPS = 1e-6
RWKV_HEAD = 64
LNX_EPS = RWKV_HEAD * 1e-5
SSM_GROUP = 16
S5_CHUNK = 16
RWKV_CHUNK = 64
RWKV_SUBCHUNKS = 3
RWKV_GROUPS = 1
FFN_ROW_TILE = 768
PROJ_ROW_TILE = 1536
LORA_BLOCK = 512
SEQ_ALIGN = 128
VMEM_LIMIT_BYTES = 56 * 1024 * 1024


def _cparams(*sem):
    return pltpu.CompilerParams(dimension_semantics=sem, vmem_limit_bytes=VMEM_LIMIT_BYTES)


def _rms(x, g):
    inv = lax.rsqrt(jnp.mean(x * x, axis=-1, keepdims=True) + NORM_EPS)
    return (x * inv) * g


def _row_tile(rows):
    for tm in (512, 256, 128, 64, 32, 16, 8):
        if rows % tm == 0:
            return tm
    raise ValueError(rows)


def _col_tile(cols):
    for tn in (512, 256, 128):
        if cols % tn == 0:
            return tn
    raise ValueError(cols)


def _ffn_kernel(x_ref, g_ref, wg_ref, wu_ref, wd_ref, o_ref, h_sc):
    f = pl.program_id(1)

    @pl.when(f == 0)
    def _():
        x = x_ref[...]
        h_sc[...] = _rms(x, g_ref[...]).astype(BF16)
        o_ref[...] = x

    h = h_sc[...]
    a = jnp.dot(h, wg_ref[...], preferred_element_type=F32)
    b = jnp.dot(h, wu_ref[...], preferred_element_type=F32)
    z = (0.5 * (a * jax.nn.sigmoid(a)) * b).astype(BF16)
    o_ref[...] += jnp.dot(z, wd_ref[...], preferred_element_type=F32)


def _ffn(x, g, wg, wu, wd, layer):
    rows, d = x.shape
    dff = wg.shape[2]
    tm = FFN_ROW_TILE if rows % FFN_ROW_TILE == 0 else _row_tile(rows)
    tf = _col_tile(dff)
    return pl.pallas_call(
        _ffn_kernel,
        grid=(rows // tm, dff // tf),
        in_specs=[
            pl.BlockSpec((tm, d), lambda i, f: (i, 0)),
            pl.BlockSpec((1, d), lambda i, f: (0, 0)),
            pl.BlockSpec((None, d, tf), lambda i, f: (layer, 0, f)),
            pl.BlockSpec((None, d, tf), lambda i, f: (layer, 0, f)),
            pl.BlockSpec((None, tf, d), lambda i, f: (layer, f, 0)),
        ],
        out_specs=pl.BlockSpec((tm, d), lambda i, f: (i, 0)),
        out_shape=jax.ShapeDtypeStruct((rows, d), F32),
        scratch_shapes=[pltpu.VMEM((tm, d), BF16)],
        compiler_params=_cparams("parallel", "arbitrary"),
        name="ffn",
    )(x, g, wg, wu, wd)


def _proj_kernel(x_ref, g_ref, w_ref, o_ref, h_sc):
    @pl.when(pl.program_id(1) == 0)
    def _():
        h_sc[...] = _rms(x_ref[...], g_ref[...]).astype(BF16)

    o_ref[...] = jnp.dot(h_sc[...], w_ref[...], preferred_element_type=F32)


def _proj_split_kernel(x_ref, g_ref, w_ref, o_ref, u_ref, h_sc, *, j0, nu):
    _proj_kernel(x_ref, g_ref, w_ref, o_ref, h_sc)
    j = pl.program_id(1)

    @pl.when(jnp.logical_and(j >= j0, j < j0 + nu))
    def _():
        u_ref[...] = o_ref[...]


def _proj(x, g, w, u_col, u_width):
    rows, d = x.shape
    n = w.shape[1]
    tm = PROJ_ROW_TILE if rows % PROJ_ROW_TILE == 0 else _row_tile(rows)
    tn = _col_tile(n)
    split = u_col % tn == 0 and u_width % tn == 0
    in_specs = [
        pl.BlockSpec((tm, d), lambda i, j: (i, 0)),
        pl.BlockSpec((1, d), lambda i, j: (0, 0)),
        pl.BlockSpec((d, tn), lambda i, j: (0, j)),
    ]
    p_spec = pl.BlockSpec((tm, tn), lambda i, j: (i, j))
    p_shape = jax.ShapeDtypeStruct((rows, n), F32)
    common = dict(grid=(rows // tm, n // tn), in_specs=in_specs,
                  scratch_shapes=[pltpu.VMEM((tm, d), BF16)],
                  compiler_params=_cparams("parallel", "arbitrary"), name="in_proj")
    if not split:
        p = pl.pallas_call(_proj_kernel, out_specs=p_spec, out_shape=p_shape, **common)(x, g, w)
        return p, p[:, u_col:u_col + u_width]
    j0, nu = u_col // tn, u_width // tn
    u_spec = pl.BlockSpec((tm, tn), lambda i, j: (i, jnp.clip(j - j0, 0, nu - 1)))
    return pl.pallas_call(
        functools.partial(_proj_split_kernel, j0=j0, nu=nu),
        out_specs=[p_spec, u_spec],
        out_shape=[p_shape, jax.ShapeDtypeStruct((rows, u_width), F32)],
        **common)(x, g, w)


def _cmul(ar, ai, br, bi):
    return ar * br - ai * bi, ar * bi + ai * br


def _cpow_int(a_re, a_im, expo, n_bits):
    shape = jnp.broadcast_shapes(a_re.shape, expo.shape)
    p_re, p_im = jnp.ones(shape, F32), jnp.zeros(shape, F32)
    for bit in range(n_bits):
        take = jnp.bitwise_and(lax.shift_right_logical(expo, bit), 1) == 1
        q_re, q_im = _cmul(p_re, p_im, a_re, a_im)
        p_re, p_im = jnp.where(take, q_re, p_re), jnp.where(take, q_im, p_im)
        if bit + 1 < n_bits:
            a_re, a_im = _cmul(a_re, a_im, a_re, a_im)
    return p_re, p_im


def _s5_param_kernel(ldt_ref, lrc_ref, lic_ref, lrr_ref, lir_ref, ctre_ref, ctim_ref,
                     btre_ref, btim_ref, bttre_ref, bttim_ref, m0_ref, e_ref, p_ref, ap_ref,
                     *, n_chunks):
    lc = S5_CHUNK
    cw = SSM_GROUP
    width = lc * cw
    n_bits = (lc - 1).bit_length()
    dt = jnp.exp(ldt_ref[...])

    def abar(lr, li):
        mag = jnp.exp(lr * dt)
        return mag * jnp.cos(li * dt), mag * jnp.sin(li * dt)

    lr_r, li_r = lrr_ref[...], lir_ref[...]
    ac_re, ac_im = abar(lrc_ref[...], lic_ref[...])
    ar_re, ar_im = abar(lr_r, li_r)

    den = lr_r * lr_r + li_r * li_r
    nr, ni = ar_re - 1.0, ar_im
    q_re = (nr * lr_r + ni * li_r) / den
    q_im = (ni * lr_r - nr * li_r) / den

    lag = lax.broadcasted_iota(jnp.int32, (1, width), 1) // cw
    ct_re, ct_im = ctre_ref[...], ctim_ref[...]
    p_re, p_im = _cpow_int(ac_re, ac_im, lag, n_bits)
    d0_re, d0_im = _cmul(ct_re, ct_im, p_re, p_im)
    bt_re, bt_im = btre_ref[...], btim_ref[...]
    bbt_re, bbt_im = _cmul(q_re, q_im, bt_re, bt_im)
    m0_ref[...] = (jnp.dot(bbt_re, d0_re, preferred_element_type=F32, precision=HIGHEST)
                   - jnp.dot(bbt_im, d0_im, preferred_element_type=F32, precision=HIGHEST))

    rowj = lax.broadcasted_iota(jnp.int32, (width, 1), 0) // cw
    e_pre, e_pim = _cpow_int(ar_re, ar_im, (lc - 1) - rowj, n_bits)
    btt_re, btt_im = bttre_ref[...], bttim_ref[...]
    bb2_re, bb2_im = _cmul(q_re, q_im, btt_re, btt_im)
    e_re, e_im = _cmul(bb2_re, bb2_im, e_pre, e_pim)
    e_ref[...] = jnp.concatenate([e_re, e_im], axis=1)

    p1_re, p1_im = _cmul(p_re, p_im, ac_re, ac_im)
    d1_re, d1_im = _cmul(ct_re, ct_im, p1_re, p1_im)
    p_ref[...] = jnp.concatenate([d1_re, -d1_im], axis=0)

    ap_ref[...] = jnp.zeros_like(ap_ref)
    pre, pim = ar_re, ar_im
    for _ in range(n_bits):
        pre, pim = _cmul(pre, pim, pre, pim)
    assert 1 << n_bits == lc
    for i, _ in enumerate(_scan_shifts(n_chunks)):
        ap_ref[2 * i:2 * i + 1, :] = jnp.concatenate([pre, pre], axis=1)
        ap_ref[2 * i + 1:2 * i + 2, :] = jnp.concatenate([-pim, pim], axis=1)
        pre, pim = _cmul(pre, pim, pre, pim)


def _scan_shifts(n_chunks):
    shifts, s = [], 1
    while s < n_chunks:
        shifts.append(s)
        s *= 2
    return shifts


def _s5_main_kernel(*refs, n_chunks, n_state):
    lc = S5_CHUNK
    cw = SSM_GROUP
    x_refs = refs[:lc]
    m0_ref, e_ref, p_ref, ap_ref, y_ref, m_sc, e_sc, p_sc, ap_sc = refs[lc:]
    lb = x_refs[0].shape[1]
    gpb = lb // cw
    width = lc * cw
    sw = 2 * n_state

    @pl.when(pl.program_id(1) == 0)
    def _():
        def iota(shape, dim):
            return lax.broadcasted_iota(jnp.int32, shape, dim)

        def spread(shape, dim_small):
            small, big = iota(shape, dim_small), iota(shape, 1 - dim_small)
            return jnp.logical_and(small // cw == big // lb, small % cw == big % cw)

        lane_expand = spread((width, lc * lb), 0).astype(BF16)
        row_expand = spread((lc * lb, width), 1).astype(BF16)

        m0w = jnp.dot(m0_ref[...].astype(BF16), lane_expand, preferred_element_type=F32)
        own = iota(m0w.shape, 0) // cw == (iota(m0w.shape, 1) % lb) // cw
        m0 = jnp.where(own, m0w, 0.0).astype(BF16)
        for t in range(lc):
            blk = m0 if t == 0 else jnp.concatenate(
                [jnp.zeros((lb, t * lb), m0.dtype), m0[:, :(lc - t) * lb]], axis=1)
            m_sc[t * lb:(t + 1) * lb, :] = blk

        e_stack = jnp.concatenate([e_ref[g] for g in range(gpb)], axis=1).astype(BF16)
        e_full = jnp.dot(row_expand, e_stack, preferred_element_type=F32)
        own = (iota(e_full.shape, 0) % lb) // cw == iota(e_full.shape, 1) // sw
        e_sc[...] = jnp.where(own, e_full, 0.0).astype(BF16)

        p_stack = p_ref[...].reshape(gpb * sw, width).astype(BF16)
        p_full = jnp.dot(p_stack, lane_expand, preferred_element_type=F32)
        own = iota(p_full.shape, 0) // sw == (iota(p_full.shape, 1) % lb) // cw
        p_sc[...] = jnp.where(own, p_full, 0.0).astype(BF16)

        ap_sc[...] = jnp.concatenate([ap_ref[g] for g in range(gpb)], axis=1)

    x = jnp.concatenate([r[...].astype(BF16) for r in x_refs], axis=1)
    y = jnp.concatenate(
        [jnp.dot(x[:, :(q + 2) * lb], m_sc[:(q + 2) * lb, q * lb:(q + 2) * lb],
                 preferred_element_type=F32) for q in range(0, lc, 2)], axis=1)
    s = jnp.dot(x, e_sc[...], preferred_element_type=F32)

    def swap_re_im(a):
        return jnp.concatenate([pltpu.roll(a[:, j * sw:(j + 1) * sw], n_state, axis=1)
                                for j in range(a.shape[1] // sw)], axis=1)

    crow = lax.broadcasted_iota(jnp.int32, (n_chunks, 1), 0)
    for i, shift in enumerate(_scan_shifts(n_chunks)):
        prev = jnp.where(crow >= shift, pltpu.roll(s, shift, axis=0), 0.0)
        s = s + ap_sc[2 * i:2 * i + 1, :] * prev + ap_sc[2 * i + 1:2 * i + 2, :] * swap_re_im(prev)
    s_start = jnp.where(crow >= 1, pltpu.roll(s, 1, axis=0), 0.0)
    y = y + jnp.dot(s_start.astype(BF16), p_sc[...], preferred_element_type=F32)
    for t in range(lc):
        y_ref[t] = y[:, t * lb:(t + 1) * lb]


def _s5_scan(u, ldt, lr, li, c_re, c_im, b_re, b_im, *, batch):
    rows, w = u.shape
    g, n = lr.shape
    cw = SSM_GROUP
    lc = S5_CHUNK
    width = lc * cw
    lb = 128
    gpb = lb // cw
    n_oct = w // lb
    n_chunks = rows // batch // lc
    ap_rows = -(-2 * len(_scan_shifts(n_chunks)) // 8) * 8
    ct_re = jnp.tile(jnp.swapaxes(c_re, 1, 2), (1, 1, lc))
    ct_im = jnp.tile(jnp.swapaxes(c_im, 1, 2), (1, 1, lc))
    bt_re = jnp.swapaxes(b_re, 1, 2)
    bt_im = jnp.swapaxes(b_im, 1, 2)
    btt_re = jnp.tile(bt_re, (1, lc, 1))
    btt_im = jnp.tile(bt_im, (1, lc, 1))
    g3 = lambda a, b: pl.BlockSpec((None, a, b), lambda i: (i, 0, 0))
    m0, e_g, p_g, ap = pl.pallas_call(
        functools.partial(_s5_param_kernel, n_chunks=n_chunks),
        grid=(g,),
        in_specs=[g3(1, 1), g3(n, 1), g3(n, 1), g3(1, n), g3(1, n),
                  g3(n, width), g3(n, width), g3(cw, n), g3(cw, n), g3(width, n), g3(width, n)],
        out_specs=[g3(cw, width), g3(width, 2 * n), g3(2 * n, width), g3(ap_rows, 2 * n)],
        out_shape=[jax.ShapeDtypeStruct((g, cw, width), F32),
                   jax.ShapeDtypeStruct((g, width, 2 * n), F32),
                   jax.ShapeDtypeStruct((g, 2 * n, width), F32),
                   jax.ShapeDtypeStruct((g, ap_rows, 2 * n), F32)],
        compiler_params=_cparams("parallel"),
        name="s5_params",
    )(ldt.reshape(g, 1, 1), lr.reshape(g, n, 1), li.reshape(g, n, 1), lr.reshape(g, 1, n),
      li.reshape(g, 1, n), ct_re, ct_im, bt_re, bt_im, btt_re, btt_im)

    m0_o = m0.reshape(n_oct, lb, width)
    e_o = e_g.reshape(n_oct, gpb, width, 2 * n)
    p_o = p_g.reshape(n_oct, gpb, 2 * n, width)
    ap_o = ap.reshape(n_oct, gpb, ap_rows, 2 * n)

    u2 = u.reshape(rows // lc, lc * w)
    xspec = lambda t: pl.BlockSpec((n_chunks, lb), lambda o, b: (b, t * n_oct + o))
    per_oct = lambda a: pl.BlockSpec((None,) + a.shape[1:], lambda o, b: (o,) + (0,) * (a.ndim - 1))
    ys = pl.pallas_call(
        functools.partial(_s5_main_kernel, n_chunks=n_chunks, n_state=n),
        grid=(n_oct, batch),
        in_specs=[xspec(t) for t in range(lc)] + [per_oct(m0_o), per_oct(e_o), per_oct(p_o),
                                                   per_oct(ap_o)],
        out_specs=pl.BlockSpec((lc, n_chunks, lb), lambda o, b: (0, b, o)),
        out_shape=jax.ShapeDtypeStruct((lc, rows // lc, w), F32),
        scratch_shapes=[pltpu.VMEM((lc * lb, lc * lb), BF16),
                        pltpu.VMEM((lc * lb, gpb * 2 * n), BF16),
                        pltpu.VMEM((gpb * 2 * n, lc * lb), BF16),
                        pltpu.VMEM((ap_rows, gpb * 2 * n), F32)],
        compiler_params=_cparams("parallel", "arbitrary"),
        name="s5_scan",
    )(*([u2] * lc), m0_o, e_o, p_o, ap_o)
    return jnp.swapaxes(ys, 0, 1).reshape(rows, w)


(_V_W0, _V_A0, _V_V0, _V_KK, _V_KA, _V_RK, _V_LNW, _V_LNB, _V_MUR, _V_MUK, _V_MUV) = range(11)
_VEC_ROWS = 16


def _dot_nt(a, b, **kw):
    return lax.dot_general(a, b, (((1,), (1,)), ((), ())), preferred_element_type=F32, **kw)


def _dot_tn(a, b, **kw):
    return lax.dot_general(a, b, (((0,), (0,)), ((), ())), preferred_element_type=F32, **kw)


def _mm(a, b):
    return jnp.dot(a.astype(BF16), b.astype(BF16), preferred_element_type=F32)


def _rwkv_solve(pairs):
    lc, pw = pairs[0][0].shape
    hd = RWKV_HEAD
    assert lc == hd and pw == 2 * hd
    lane = lax.broadcasted_iota(jnp.int32, (lc, pw), 1)
    row = lax.broadcasted_iota(jnp.int32, (lc, pw), 0)
    first = lane < hd
    col = jnp.where(first, lane, lane - hd)
    strict2 = row > col
    incl2 = row >= col
    ak_mask = jnp.logical_and(strict2, jnp.logical_not(first))
    zeros = jnp.zeros((lc, pw), F32)
    heads = [(p, e) for p in range(len(pairs)) for e in range(2)]

    gms = []
    for rt, at, bt, kt, v in pairs:
        ar = jnp.concatenate([jnp.where(first, at, 0.0), jnp.where(first, rt, 0.0),
                              jnp.where(first, 0.0, at), jnp.where(first, 0.0, rt)], axis=0)
        gms.append(_dot_nt(ar.astype(BF16), jnp.concatenate([bt, kt], axis=0).astype(BF16)))
    ga = [gms[p][2 * e * lc:(2 * e + 1) * lc] for p, e in heads]
    gr = [gms[p][(2 * e + 1) * lc:(2 * e + 2) * lc] for p, e in heads]
    vr = [pltpu.roll(pr[4], hd, axis=1) for pr in pairs]
    vz = [jnp.concatenate([zeros, r], axis=0).astype(BF16) for r in vr]

    nk = [jnp.where(strict2[:, :lc], g[:, :lc], 0.0).astype(BF16) for g in ga]
    y = []
    for i, (p, e) in enumerate(heads):
        akv = _mm(jnp.where(ak_mask, ga[i], 0.0), vz[p])
        y.append(jnp.where(first, pairs[p][1], akv) if e == 0 else jnp.where(first, akv, pairs[p][1]))
    steps = max(1, (lc - 1).bit_length())
    for i in range(steps):
        y = [yy + _mm(nn, yy) for nn, yy in zip(nk, y)]
        if i + 1 < steps:
            nk = [_mm(nn, nn).astype(BF16) for nn in nk]
    rz = []
    for i, (p, e) in enumerate(heads):
        zv = jnp.where(first, 0.0, vr[p]) if e == 0 else jnp.where(first, vr[p], 0.0)
        rz.append(_mm(jnp.where(incl2, gr[i], 0.0), jnp.concatenate([y[i], zv], axis=0)))

    outs = []
    for p, (rt, at, bt, kt, v) in enumerate(pairs):
        y0, y1, rz0, rz1 = y[2 * p], y[2 * p + 1], rz[2 * p], rz[2 * p + 1]
        outs.append((jnp.where(first, y0, y1),
                     rt + jnp.where(first, rz0, rz1),
                     pltpu.roll(jnp.where(first, y1, y0), hd, axis=1),
                     pltpu.roll(jnp.where(first, rz1, rz0), hd, axis=1)))
    return outs


def _rwkv_state(solved, tails, states):
    lc, pw = solved[0][0].shape
    hd = RWKV_HEAD
    uos = [_dot_nt(jnp.concatenate([x1, z1], axis=0).astype(BF16), s0.astype(BF16))
           for (x1, z1, x2, z2), s0 in zip(solved, states)]
    si = lax.broadcasted_iota(jnp.int32, (pw, pw), 0)
    sj = lax.broadcasted_iota(jnp.int32, (pw, pw), 1)
    same_head = (si < hd) == (sj < hd)
    outs = []
    for uo, (x1, z1, x2, z2), (v, bh, kh, wl), s0 in zip(uos, solved, tails, states):
        u = uo[:lc] + x2
        o = uo[lc:] + z2
        upd = _dot_tn(jnp.concatenate([u, v], axis=0).astype(BF16),
                      jnp.concatenate([bh, kh], axis=0).astype(BF16))
        outs.append((o, s0 * wl + jnp.where(same_head, upd, 0.0)))
    return outs


def _rwkv_kernel(*refs, has_vres, n_heads):
    if has_vres:
        (r_ref, k_ref, v_ref, l_ref, vf_ref, vec_ref, mul_ref, w2_ref, a2_ref, g2_ref, v2_ref,
         o_ref, s_sc, pr_sc, pk_sc, pv_sc, pl_sc, big_sc, on_sc, bonus_sc, gate_sc) = refs
        vout_ref = None
    else:
        (r_ref, k_ref, v_ref, l_ref, vec_ref, mul_ref, w2_ref, a2_ref, g2_ref,
         o_ref, vout_ref, s_sc, pr_sc, pk_sc, pv_sc, pl_sc, big_sc, on_sc, bonus_sc, gate_sc) = refs
    nr, width = r_ref.shape
    lc = RWKV_CHUNK
    n_sub = nr // lc
    hd = RWKV_HEAD
    pw = 2 * hd

    @pl.when(pl.program_id(1) == 0)
    def _():
        s_sc[...] = jnp.zeros_like(s_sc)
        pr_sc[...] = jnp.zeros_like(pr_sc)
        pk_sc[...] = jnp.zeros_like(pk_sc)
        pv_sc[...] = jnp.zeros_like(pv_sc)
        pl_sc[...] = jnp.zeros_like(pl_sc)

    def vec(i):
        return vec_ref[i:i + 1, :]

    n_grp = RWKV_GROUPS if n_sub % RWKV_GROUPS == 0 else 1
    cpg = n_sub // n_grp
    g_rows = cpg * lc

    def prologue(g, prevs):
        rows = slice(g * g_rows, (g + 1) * g_rows)
        row = lax.broadcasted_iota(jnp.int32, (g_rows, 1), 0)

        def shift_mix(x_ref, prev, mu):
            x = x_ref[rows, :]
            before = jnp.where(row == 0, prev, pltpu.roll(x, 1, axis=0))
            return x + mu * (before - x), x[g_rows - 1:g_rows, :]

        xr, last_r = shift_mix(r_ref, prevs[0], vec(_V_MUR))
        xk, last_k = shift_mix(k_ref, prevs[1], vec(_V_MUK))
        xv, last_v = shift_mix(v_ref, prevs[2], vec(_V_MUV))
        xl, last_l = shift_mix(l_ref, prevs[3], mul_ref[...])
        xwa, xg, xvr = xl[:, 0:128], xl[:, 128:384], xl[:, 384:512]

        logw = (-math.exp(-0.5)) * jax.nn.sigmoid(vec(_V_W0) + _mm(jnp.tanh(xwa), w2_ref[...]))
        iclr = jax.nn.sigmoid(vec(_V_A0) + _mm(xwa, a2_ref[...]))
        gate_sc[rows, :] = _mm(jax.nn.sigmoid(xg), g2_ref[...])
        if has_vres:
            v = xv + (vf_ref[rows, :] - xv) * jax.nn.sigmoid(vec(_V_V0) + _mm(xvr, v2_ref[...]))
        else:
            v = xv
            vout_ref[rows, :] = xv
        k = xk * (1.0 + (iclr - 1.0) * vec(_V_KA))

        ti = lax.broadcasted_iota(jnp.int32, (g_rows, g_rows), 0)
        tj = lax.broadcasted_iota(jnp.int32, (g_rows, g_rows), 1)
        tri = jnp.logical_and(ti >= tj, ti // lc == tj // lc)
        logw_hi = logw.astype(BF16)
        logw_lo = (logw - logw_hi.astype(F32)).astype(BF16)
        tri = tri.astype(BF16)
        cum = (jnp.dot(tri, logw_hi, preferred_element_type=F32)
               + jnp.dot(tri, logw_lo, preferred_element_type=F32))
        cum_last = cum[lc - 1:lc, :]
        for h in range(1, cpg):
            cum_last = jnp.where(row // lc == h, cum[(h + 1) * lc - 1:(h + 1) * lc, :], cum_last)
        dec_inv = jnp.exp(-cum)
        dec_tail = jnp.exp(cum_last - cum)

        big_sc[0, rows, :] = xr * jnp.exp(cum)
        big_sc[1, rows, :] = xk * vec(_V_KK)
        big_sc[2, rows, :] = iclr
        big_sc[3, rows, :] = k * dec_inv
        big_sc[4, rows, :] = v
        big_sc[5, rows, :] = jnp.exp(cum - logw)
        big_sc[6, rows, :] = k * dec_tail
        big_sc[7, rows, :] = xr * k * vec(_V_RK)
        big_sc[8, rows, :] = dec_inv
        big_sc[9, rows, :] = dec_tail
        big_sc[10, rows, :] = jnp.broadcast_to(jnp.exp(cum_last), (g_rows, width))
        return last_r, last_k, last_v, last_l

    prevs = (pr_sc[...], pk_sc[...], pv_sc[...], pl_sc[...])
    for g in range(n_grp):
        prevs = prologue(g, prevs)
    pr_sc[...], pk_sc[...], pv_sc[...], pl_sc[...] = prevs

    first = lax.broadcasted_iota(jnp.int32, (lc, pw), 1) < hd

    def head_sum(x):
        s0 = jnp.sum(jnp.where(first, x, 0.0), axis=-1, keepdims=True)
        s1 = jnp.sum(jnp.where(first, 0.0, x), axis=-1, keepdims=True)
        return jnp.where(first, s0, s1)

    n_pairs = n_heads // 2
    states = [s_sc[hp] for hp in range(n_pairs)]
    for g in range(n_grp):
        chunks = range(g * cpg, (g + 1) * cpg)
        ins, tails = [], []
        for h in chunks:
            for hp in range(n_pairs):
                rows, lanes = slice(h * lc, (h + 1) * lc), slice(hp * pw, (hp + 1) * pw)
                rt, kkraw, iclr_p, kt, v_p, dex, kh, rk, dinv, dtail, wl = [
                    big_sc[i, rows, lanes] for i in range(11)]
                kkn = kkraw * lax.rsqrt(head_sum(kkraw * kkraw) + 1e-12)
                b = kkn * iclr_p
                ins.append((rt, -kkn * dex, b * dinv, kt, v_p))
                tails.append((v_p, b * dtail, kh, wl[0:1]))
                bonus_sc[rows, lanes] = head_sum(rk) * v_p
        solved = _rwkv_solve(ins)
        for j, h in enumerate(chunks):
            sel = slice(j * n_pairs, (j + 1) * n_pairs)
            outs = _rwkv_state(solved[sel], tails[sel], states)
            states = [s_new for _, s_new in outs]
            for hp, (o, _) in enumerate(outs):
                cen = o - head_sum(o) * (1.0 / hd)
                var = head_sum(cen * cen) * (1.0 / hd)
                on_sc[h * lc:(h + 1) * lc, hp * pw:(hp + 1) * pw] = cen * lax.rsqrt(var + LNX_EPS)
    for hp in range(n_pairs):
        s_sc[hp] = states[hp]
    o_ref[...] = (on_sc[...] * vec(_V_LNW) + vec(_V_LNB) + bonus_sc[...]) * gate_sc[...]


def _rwkv(p, vfirst, vecs, mu_l, w2, a2, g2p, v2p, *, batch, width, col0, lora_col):
    rows = p.shape[0]
    tp = rows // batch
    n_sub = RWKV_SUBCHUNKS if tp % (RWKV_SUBCHUNKS * RWKV_CHUNK) == 0 else 1
    lc = n_sub * RWKV_CHUNK
    nck = tp // lc
    n_heads = width // RWKV_HEAD
    has_vres = vfirst is not None
    rowblk = lambda b, c: b * nck + c
    cb = col0 // width
    tok = lambda j: pl.BlockSpec((lc, width), lambda b, c: (rowblk(b, c), j))
    full = lambda a: pl.BlockSpec(a.shape, lambda b, c: (0,) * a.ndim)
    in_specs = [tok(cb), tok(cb + 1), tok(cb + 2),
                pl.BlockSpec((lc, LORA_BLOCK), lambda b, c: (rowblk(b, c), lora_col // LORA_BLOCK))]
    args = [p, p, p, p]
    if has_vres:
        in_specs.append(tok(0))
        args.append(vfirst)
    in_specs += [full(vecs), full(mu_l), full(w2), full(a2), full(g2p)]
    args += [vecs, mu_l, w2, a2, g2p]
    if has_vres:
        in_specs.append(full(v2p))
        args.append(v2p)
    out_shape = [jax.ShapeDtypeStruct((rows, width), F32)]
    out_specs = [tok(0)]
    if not has_vres:
        out_shape.append(jax.ShapeDtypeStruct((rows, width), F32))
        out_specs.append(tok(0))
    outs = pl.pallas_call(
        functools.partial(_rwkv_kernel, has_vres=has_vres, n_heads=n_heads),
        grid=(batch, nck),
        in_specs=in_specs,
        out_specs=out_specs,
        out_shape=out_shape,
        scratch_shapes=[pltpu.VMEM((n_heads // 2, 2 * RWKV_HEAD, 2 * RWKV_HEAD), F32),
                        pltpu.VMEM((1, width), F32), pltpu.VMEM((1, width), F32),
                        pltpu.VMEM((1, width), F32), pltpu.VMEM((1, LORA_BLOCK), F32),
                        pltpu.VMEM((11, lc, width), F32), pltpu.VMEM((lc, width), F32),
                        pltpu.VMEM((lc, width), F32), pltpu.VMEM((lc, width), F32)],
        compiler_params=_cparams("parallel", "arbitrary"),
        name="rwkv7",
    )(*args)
    return outs if not has_vres else (outs[0], None)


def _merge_kernel(x_ref, ga_ref, gb_ref, u_ref, ys_ref, yb_ref, d_ref, wglu_ref, wua_ref, wub_ref,
                  wo_ref, o_ref):
    y = ys_ref[...] + d_ref[...] * u_ref[...]
    y = jax.nn.gelu(y, approximate=True)
    ya = y * jax.nn.sigmoid(jnp.dot(y.astype(BF16), wglu_ref[...], preferred_element_type=F32))
    up_a = jnp.dot(ya.astype(BF16), wua_ref[...], preferred_element_type=F32)
    up_b = jnp.dot(yb_ref[...].astype(BF16), wub_ref[...], preferred_element_type=F32)
    merged = jax.nn.sigmoid(ga_ref[...]) * up_a + jax.nn.sigmoid(gb_ref[...]) * up_b
    o_ref[...] = x_ref[...] + jnp.dot(merged.astype(BF16), wo_ref[...], preferred_element_type=F32)


def _merge(x, p, ys, yb, d_skip, wglu, wua, wub, wo, *, u_col, layer):
    rows, d = x.shape
    w = ys.shape[1]
    tm = min(256, _row_tile(rows))
    row = lambda width, j: pl.BlockSpec((tm, width), lambda i: (i, j))
    const = lambda a: pl.BlockSpec((None,) + a.shape[1:], lambda i: (layer, 0, 0),
                                   pipeline_mode=pl.Buffered(1))
    return pl.pallas_call(
        _merge_kernel,
        grid=(rows // tm,),
        in_specs=[row(d, 0), row(d, 0), row(d, 1), row(w, u_col // w), row(w, 0), row(w, 0),
                  pl.BlockSpec(d_skip.shape, lambda i: (0, 0)),
                  const(wglu), const(wua), const(wub), const(wo)],
        out_specs=row(d, 0),
        out_shape=jax.ShapeDtypeStruct((rows, d), F32),
        compiler_params=_cparams("parallel"),
        name="merge",
    )(x, p, p, p, ys, yb, d_skip, wglu, wua, wub, wo)


def _norm_kernel(x_ref, g_ref, o_ref):
    o_ref[...] = _rms(x_ref[...], g_ref[...])


def _final_norm(x, g):
    rows, d = x.shape
    tm = _row_tile(rows)
    return pl.pallas_call(
        _norm_kernel,
        grid=(rows // tm,),
        in_specs=[pl.BlockSpec((tm, d), lambda i: (i, 0)), pl.BlockSpec((1, d), lambda i: (0, 0))],
        out_specs=pl.BlockSpec((tm, d), lambda i: (i, 0)),
        out_shape=jax.ShapeDtypeStruct((rows, d), F32),
        compiler_params=_cparams("parallel"),
        name="final_norm",
    )(x, g)


def _pad_cols(a, width):
    return jnp.pad(a, ((0, 0), (0, width - a.shape[1])))


def _pad_rows(a, height):
    return jnp.pad(a, ((0, height - a.shape[0]), (0, 0)))


def kernel(x, meta_tokens, ffn1_norm, ffn1_w_gate, ffn1_w_up, ffn1_w_down, mix_norm, w_in_first, w_in_rest, mu_shift, mu_vres, ssm_lambda_re, ssm_lambda_im, ssm_log_dt, ssm_b_re, ssm_b_im, ssm_c_re, ssm_c_im, ssm_d, ssm_w_glu, rwkv_w0, rwkv_w2, rwkv_a0, rwkv_a2, rwkv_v0, rwkv_v2, rwkv_g2, rwkv_k_k, rwkv_k_a, rwkv_r_k, rwkv_lnx_w, rwkv_lnx_b, w_up_ssm, w_up_rwkv, w_out, ffn2_norm, ffn2_w_gate, ffn2_w_up, ffn2_w_down, final_norm):
    bsz, seq, d = x.shape
    n_meta = meta_tokens.shape[0]
    depth = ffn1_norm.shape[0]
    w = ssm_d.shape[1]
    n_groups = ssm_b_re.shape[1]
    ld, la, lg, lv = rwkv_w2.shape[1], rwkv_a2.shape[1], rwkv_g2.shape[1], rwkv_v2.shape[1]
    assert w == rwkv_w0.shape[1] and 2 * w == d and w % 128 == 0
    assert n_groups * SSM_GROUP == w and ld + la <= 128 and lg <= 256 and lv <= 128
    t_len = n_meta + seq
    tp = -(-t_len // SEQ_ALIGN) * SEQ_ALIGN
    rows = bsz * tp

    meta = jnp.broadcast_to(meta_tokens[None].astype(x.dtype), (bsz, n_meta, d))
    h_res = jnp.concatenate([meta, x, jnp.zeros((bsz, tp - t_len, d), x.dtype)], axis=1)
    h_res = h_res.reshape(rows, d)

    u_col = 2 * d
    rkv_col = u_col + w
    lora_col = rkv_col + 3 * w
    c_w = 4 * w
    c_g = c_w + ld + la
    c_ga = c_g + lg
    c_gb = c_ga + d
    p_common = c_gb + d

    ffn1_w = [a.astype(BF16) for a in (ffn1_w_gate, ffn1_w_up, ffn1_w_down)]
    ffn2_w = [a.astype(BF16) for a in (ffn2_w_gate, ffn2_w_up, ffn2_w_down)]
    merge_w = [a.astype(BF16) for a in (ssm_w_glu, w_up_ssm, w_up_rwkv, w_out)]

    v_first = None
    for i in range(depth):
        h_res = _ffn(h_res, ffn1_norm[i][None], *ffn1_w, i)

        w_in = w_in_first if i == 0 else w_in_rest[i - 1]
        blocks = [w_in[:, c_ga:c_gb], w_in[:, c_gb:p_common], w_in[:, :c_w],
                  _pad_cols(w_in[:, c_w:c_g], 128), _pad_cols(w_in[:, c_g:c_ga], 256),
                  _pad_cols(w_in[:, p_common:], 128)]
        p, u = _proj(h_res, mix_norm[i][None], jnp.concatenate(blocks, axis=1).astype(BF16),
                     u_col, w)

        ys = _s5_scan(u, ssm_log_dt[i], ssm_lambda_re[i], ssm_lambda_im[i],
                      ssm_c_re[i], ssm_c_im[i], ssm_b_re[i], ssm_b_im[i], batch=bsz)

        mu = mu_shift[i]
        zero_w = jnp.zeros((w,), F32)
        vecs = jnp.stack([rwkv_w0[i], rwkv_a0[i], rwkv_v0[i - 1] if i else zero_w, rwkv_k_k[i],
                          rwkv_k_a[i], rwkv_r_k[i].reshape(w), rwkv_lnx_w[i], rwkv_lnx_b[i],
                          mu[0:w], mu[w:2 * w], mu[2 * w:3 * w]])
        vecs = _pad_rows(vecs, _VEC_ROWS)
        mu_v = mu_vres[i - 1] if i else jnp.zeros((lv,), F32)
        mu_l = jnp.concatenate([jnp.pad(mu[3 * w:3 * w + ld + la], (0, 128 - ld - la)),
                                jnp.pad(mu[3 * w + ld + la:], (0, 256 - lg)),
                                jnp.pad(mu_v, (0, 128 - lv))])[None]
        w2p = _pad_rows(rwkv_w2[i], 128).astype(BF16)
        a2p = jnp.pad(rwkv_a2[i], ((ld, 128 - ld - la), (0, 0))).astype(BF16)
        g2p = _pad_rows(rwkv_g2[i], 256).astype(BF16)
        v2p = _pad_rows(rwkv_v2[i - 1], 128).astype(BF16) if i else None
        yb, v_new = _rwkv(p, v_first, vecs, mu_l, w2p, a2p, g2p, v2p,
                          batch=bsz, width=w, col0=rkv_col, lora_col=lora_col)
        if i == 0:
            v_first = v_new

        h_res = _merge(h_res, p, ys, yb, ssm_d[i][None], *merge_w, u_col=u_col, layer=i)

        h_res = _ffn(h_res, ffn2_norm[i][None], *ffn2_w, i)

    out = _final_norm(h_res, final_norm[None]).reshape(bsz, tp, d)
    return out[:, n_meta:t_len]
```

```python
import functools
import math

import jax
import jax.numpy as jnp
from jax import lax
from jax.experimental import pallas as pl
from jax.experimental.pallas import tpu as pltpu

F32 = jnp.float32
BF16 = jnp.bfloat16
HIGHEST = lax.Precision.HIGHEST

NORM_EPS = 1e-6
RWKV_HEAD = 64
LNX_EPS = RWKV_HEAD * 1e-5
SSM_GROUP = 16
S5_CHUNK = 16
RWKV_CHUNK = 64
RWKV_SUBCHUNKS = 3
RWKV_GROUPS = 1
FFN_ROW_TILE = 1056
MERGE_ROW_TILE = 384
PROJ_ROW_TILE = 1536
LORA_BLOCK = 512
SEQ_ALIGN = 128
VMEM_LIMIT_BYTES = 56 * 1024 * 1024


def _cparams(*sem):
    return pltpu.CompilerParams(dimension_semantics=sem, vmem_limit_bytes=VMEM_LIMIT_BYTES)


def _rms(x, g):
    inv = lax.rsqrt(jnp.mean(x * x, axis=-1, keepdims=True) + NORM_EPS)
    return (x * inv) * g


def _row_tile(rows):
    for tm in (512, 256, 128, 64, 32, 16, 8):
        if rows % tm == 0:
            return tm
    raise ValueError(rows)


def _col_tile(cols):
    for tn in (512, 256, 128):
        if cols % tn == 0:
            return tn
    raise ValueError(cols)


def _ffn_kernel(x_ref, g_ref, wg_ref, wu_ref, wd_ref, o_ref, h_sc):
    f = pl.program_id(1)

    @pl.when(f == 0)
    def _():
        x = x_ref[...]
        h_sc[...] = _rms(x, g_ref[...]).astype(BF16)
        o_ref[...] = x

    h = h_sc[...]
    a = jnp.dot(h, wg_ref[...], preferred_element_type=F32)
    b = jnp.dot(h, wu_ref[...], preferred_element_type=F32)
    z = (0.5 * (a * jax.nn.sigmoid(a)) * b).astype(BF16)
    o_ref[...] += jnp.dot(z, wd_ref[...], preferred_element_type=F32)


def _ffn(x, g, wg, wu, wd, layer):
    rows, d = x.shape
    dff = wg.shape[2]
    tm = FFN_ROW_TILE if rows % FFN_ROW_TILE == 0 else _row_tile(rows)
    tf = _col_tile(dff)
    return pl.pallas_call(
        _ffn_kernel,
        grid=(rows // tm, dff // tf),
        in_specs=[
            pl.BlockSpec((tm, d), lambda i, f: (i, 0)),
            pl.BlockSpec((1, d), lambda i, f: (0, 0)),
            pl.BlockSpec((None, d, tf), lambda i, f: (layer, 0, f)),
            pl.BlockSpec((None, d, tf), lambda i, f: (layer, 0, f)),
            pl.BlockSpec((None, tf, d), lambda i, f: (layer, f, 0)),
        ],
        out_specs=pl.BlockSpec((tm, d), lambda i, f: (i, 0)),
        out_shape=jax.ShapeDtypeStruct((rows, d), F32),
        scratch_shapes=[pltpu.VMEM((tm, d), BF16)],
        compiler_params=_cparams("parallel", "arbitrary"),
        name="ffn",
    )(x, g, wg, wu, wd)


def _proj_kernel(x_ref, g_ref, w_ref, o_ref, h_sc):
    @pl.when(pl.program_id(1) == 0)
    def _():
        h_sc[...] = _rms(x_ref[...], g_ref[...]).astype(BF16)

    o_ref[...] = jnp.dot(h_sc[...], w_ref[...], preferred_element_type=F32)


def _proj_split_kernel(x_ref, g_ref, w_ref, o_ref, u_ref, h_sc, *, j0, nu):
    _proj_kernel(x_ref, g_ref, w_ref, o_ref, h_sc)
    j = pl.program_id(1)

    @pl.when(jnp.logical_and(j >= j0, j < j0 + nu))
    def _():
        u_ref[...] = o_ref[...]


def _proj(x, g, w, u_col, u_width):
    rows, d = x.shape
    n = w.shape[1]
    tm = PROJ_ROW_TILE if rows % PROJ_ROW_TILE == 0 else _row_tile(rows)
    tn = _col_tile(n)
    split = u_col % tn == 0 and u_width % tn == 0
    in_specs = [
        pl.BlockSpec((tm, d), lambda i, j: (i, 0)),
        pl.BlockSpec((1, d), lambda i, j: (0, 0)),
        pl.BlockSpec((d, tn), lambda i, j: (0, j)),
    ]
    p_spec = pl.BlockSpec((tm, tn), lambda i, j: (i, j))
    p_shape = jax.ShapeDtypeStruct((rows, n), F32)
    common = dict(grid=(rows // tm, n // tn), in_specs=in_specs,
                  scratch_shapes=[pltpu.VMEM((tm, d), BF16)],
                  compiler_params=_cparams("parallel", "arbitrary"), name="in_proj")
    if not split:
        p = pl.pallas_call(_proj_kernel, out_specs=p_spec, out_shape=p_shape, **common)(x, g, w)
        return p, p[:, u_col:u_col + u_width]
    j0, nu = u_col // tn, u_width // tn
    u_spec = pl.BlockSpec((tm, tn), lambda i, j: (i, jnp.clip(j - j0, 0, nu - 1)))
    return pl.pallas_call(
        functools.partial(_proj_split_kernel, j0=j0, nu=nu),
        out_specs=[p_spec, u_spec],
        out_shape=[p_shape, jax.ShapeDtypeStruct((rows, u_width), F32)],
        **common)(x, g, w)


def _cmul(ar, ai, br, bi):
    return ar * br - ai * bi, ar * bi + ai * br


def _cpow_int(a_re, a_im, expo, n_bits):
    shape = jnp.broadcast_shapes(a_re.shape, expo.shape)
    p_re, p_im = jnp.ones(shape, F32), jnp.zeros(shape, F32)
    for bit in range(n_bits):
        take = jnp.bitwise_and(lax.shift_right_logical(expo, bit), 1) == 1
        q_re, q_im = _cmul(p_re, p_im, a_re, a_im)
        p_re, p_im = jnp.where(take, q_re, p_re), jnp.where(take, q_im, p_im)
        if bit + 1 < n_bits:
            a_re, a_im = _cmul(a_re, a_im, a_re, a_im)
    return p_re, p_im


def _s5_param_kernel(ldt_ref, lrc_ref, lic_ref, lrr_ref, lir_ref, ctre_ref, ctim_ref,
                     btre_ref, btim_ref, bttre_ref, bttim_ref, m0_ref, e_ref, p_ref, ap_ref,
                     *, n_chunks):
    lc = S5_CHUNK
    cw = SSM_GROUP
    width = lc * cw
    n_bits = (lc - 1).bit_length()
    dt = jnp.exp(ldt_ref[...])

    def abar(lr, li):
        mag = jnp.exp(lr * dt)
        return mag * jnp.cos(li * dt), mag * jnp.sin(li * dt)

    lr_r, li_r = lrr_ref[...], lir_ref[...]
    ac_re, ac_im = abar(lrc_ref[...], lic_ref[...])
    ar_re, ar_im = abar(lr_r, li_r)

    den = lr_r * lr_r + li_r * li_r
    nr, ni = ar_re - 1.0, ar_im
    q_re = (nr * lr_r + ni * li_r) / den
    q_im = (ni * lr_r - nr * li_r) / den

    lag = lax.broadcasted_iota(jnp.int32, (1, width), 1) // cw
    ct_re, ct_im = ctre_ref[...], ctim_ref[...]
    p_re, p_im = _cpow_int(ac_re, ac_im, lag, n_bits)
    d0_re, d0_im = _cmul(ct_re, ct_im, p_re, p_im)
    bt_re, bt_im = btre_ref[...], btim_ref[...]
    bbt_re, bbt_im = _cmul(q_re, q_im, bt_re, bt_im)
    m0_ref[...] = (jnp.dot(bbt_re, d0_re, preferred_element_type=F32, precision=HIGHEST)
                   - jnp.dot(bbt_im, d0_im, preferred_element_type=F32, precision=HIGHEST))

    rowj = lax.broadcasted_iota(jnp.int32, (width, 1), 0) // cw
    e_pre, e_pim = _cpow_int(ar_re, ar_im, (lc - 1) - rowj, n_bits)
    btt_re, btt_im = bttre_ref[...], bttim_ref[...]
    bb2_re, bb2_im = _cmul(q_re, q_im, btt_re, btt_im)
    e_re, e_im = _cmul(bb2_re, bb2_im, e_pre, e_pim)
    e_ref[...] = jnp.concatenate([e_re, e_im], axis=1)

    p1_re, p1_im = _cmul(p_re, p_im, ac_re, ac_im)
    d1_re, d1_im = _cmul(ct_re, ct_im, p1_re, p1_im)
    p_ref[...] = jnp.concatenate([d1_re, -d1_im], axis=0)

    ap_ref[...] = jnp.zeros_like(ap_ref)
    pre, pim = ar_re, ar_im
    for _ in range(n_bits):
        pre, pim = _cmul(pre, pim, pre, pim)
    assert 1 << n_bits == lc
    for i, _ in enumerate(_scan_shifts(n_chunks)):
        ap_ref[2 * i:2 * i + 1, :] = jnp.concatenate([pre, pre], axis=1)
        ap_ref[2 * i + 1:2 * i + 2, :] = jnp.concatenate([-pim, pim], axis=1)
        pre, pim = _cmul(pre, pim, pre, pim)


def _scan_shifts(n_chunks):
    shifts, s = [], 1
    while s < n_chunks:
        shifts.append(s)
        s *= 2
    return shifts


def _s5_main_kernel(*refs, n_chunks, n_state):
    lc = S5_CHUNK
    cw = SSM_GROUP
    x_refs = refs[:lc]
    m0_ref, e_ref, p_ref, ap_ref, y_ref, m_sc, e_sc, p_sc, ap_sc = refs[lc:]
    lb = x_refs[0].shape[1]
    gpb = lb // cw
    width = lc * cw
    sw = 2 * n_state

    @pl.when(pl.program_id(1) == 0)
    def _():
        def iota(shape, dim):
            return lax.broadcasted_iota(jnp.int32, shape, dim)

        def spread(shape, dim_small):
            small, big = iota(shape, dim_small), iota(shape, 1 - dim_small)
            return jnp.logical_and(small // cw == big // lb, small % cw == big % cw)

        lane_expand = spread((width, lc * lb), 0).astype(BF16)
        row_expand = spread((lc * lb, width), 1).astype(BF16)

        m0w = jnp.dot(m0_ref[...].astype(BF16), lane_expand, preferred_element_type=F32)
        own = iota(m0w.shape, 0) // cw == (iota(m0w.shape, 1) % lb) // cw
        m0 = jnp.where(own, m0w, 0.0).astype(BF16)
        for t in range(lc):
            blk = m0 if t == 0 else jnp.concatenate(
                [jnp.zeros((lb, t * lb), m0.dtype), m0[:, :(lc - t) * lb]], axis=1)
            m_sc[t * lb:(t + 1) * lb, :] = blk

        e_stack = jnp.concatenate([e_ref[g] for g in range(gpb)], axis=1).astype(BF16)
        e_full = jnp.dot(row_expand, e_stack, preferred_element_type=F32)
        own = (iota(e_full.shape, 0) % lb) // cw == iota(e_full.shape, 1) // sw
        e_sc[...] = jnp.where(own, e_full, 0.0).astype(BF16)

        p_stack = p_ref[...].reshape(gpb * sw, width).astype(BF16)
        p_full = jnp.dot(p_stack, lane_expand, preferred_element_type=F32)
        own = iota(p_full.shape, 0) // sw == (iota(p_full.shape, 1) % lb) // cw
        p_sc[...] = jnp.where(own, p_full, 0.0).astype(BF16)

        ap_sc[...] = jnp.concatenate([ap_ref[g] for g in range(gpb)], axis=1)

    x = jnp.concatenate([r[...].astype(BF16) for r in x_refs], axis=1)
    y = jnp.concatenate(
        [jnp.dot(x[:, :(q + 2) * lb], m_sc[:(q + 2) * lb, q * lb:(q + 2) * lb],
                 preferred_element_type=F32) for q in range(0, lc, 2)], axis=1)
    s = jnp.dot(x, e_sc[...], preferred_element_type=F32)

    def swap_re_im(a):
        return jnp.concatenate([pltpu.roll(a[:, j * sw:(j + 1) * sw], n_state, axis=1)
                                for j in range(a.shape[1] // sw)], axis=1)

    crow = lax.broadcasted_iota(jnp.int32, (n_chunks, 1), 0)
    for i, shift in enumerate(_scan_shifts(n_chunks)):
        prev = jnp.where(crow >= shift, pltpu.roll(s, shift, axis=0), 0.0)
        s = s + ap_sc[2 * i:2 * i + 1, :] * prev + ap_sc[2 * i + 1:2 * i + 2, :] * swap_re_im(prev)
    s_start = jnp.where(crow >= 1, pltpu.roll(s, 1, axis=0), 0.0)
    y = y + jnp.dot(s_start.astype(BF16), p_sc[...], preferred_element_type=F32)
    for t in range(lc):
        y_ref[t] = y[:, t * lb:(t + 1) * lb]


def _s5_scan(u, ldt, lr, li, c_re, c_im, b_re, b_im, *, batch):
    rows, w = u.shape
    g, n = lr.shape
    cw = SSM_GROUP
    lc = S5_CHUNK
    width = lc * cw
    lb = 128
    gpb = lb // cw
    n_oct = w // lb
    n_chunks = rows // batch // lc
    ap_rows = -(-2 * len(_scan_shifts(n_chunks)) // 8) * 8
    ct_re = jnp.tile(jnp.swapaxes(c_re, 1, 2), (1, 1, lc))
    ct_im = jnp.tile(jnp.swapaxes(c_im, 1, 2), (1, 1, lc))
    bt_re = jnp.swapaxes(b_re, 1, 2)
    bt_im = jnp.swapaxes(b_im, 1, 2)
    btt_re = jnp.tile(bt_re, (1, lc, 1))
    btt_im = jnp.tile(bt_im, (1, lc, 1))
    g3 = lambda a, b: pl.BlockSpec((None, a, b), lambda i: (i, 0, 0))
    m0, e_g, p_g, ap = pl.pallas_call(
        functools.partial(_s5_param_kernel, n_chunks=n_chunks),
        grid=(g,),
        in_specs=[g3(1, 1), g3(n, 1), g3(n, 1), g3(1, n), g3(1, n),
                  g3(n, width), g3(n, width), g3(cw, n), g3(cw, n), g3(width, n), g3(width, n)],
        out_specs=[g3(cw, width), g3(width, 2 * n), g3(2 * n, width), g3(ap_rows, 2 * n)],
        out_shape=[jax.ShapeDtypeStruct((g, cw, width), F32),
                   jax.ShapeDtypeStruct((g, width, 2 * n), F32),
                   jax.ShapeDtypeStruct((g, 2 * n, width), F32),
                   jax.ShapeDtypeStruct((g, ap_rows, 2 * n), F32)],
        compiler_params=_cparams("parallel"),
        name="s5_params",
    )(ldt.reshape(g, 1, 1), lr.reshape(g, n, 1), li.reshape(g, n, 1), lr.reshape(g, 1, n),
      li.reshape(g, 1, n), ct_re, ct_im, bt_re, bt_im, btt_re, btt_im)

    m0_o = m0.reshape(n_oct, lb, width)
    e_o = e_g.reshape(n_oct, gpb, width, 2 * n)
    p_o = p_g.reshape(n_oct, gpb, 2 * n, width)
    ap_o = ap.reshape(n_oct, gpb, ap_rows, 2 * n)

    u2 = u.reshape(rows // lc, lc * w)
    xspec = lambda t: pl.BlockSpec((n_chunks, lb), lambda o, b: (b, t * n_oct + o))
    per_oct = lambda a: pl.BlockSpec((None,) + a.shape[1:], lambda o, b: (o,) + (0,) * (a.ndim - 1))
    ys = pl.pallas_call(
        functools.partial(_s5_main_kernel, n_chunks=n_chunks, n_state=n),
        grid=(n_oct, batch),
        in_specs=[xspec(t) for t in range(lc)] + [per_oct(m0_o), per_oct(e_o), per_oct(p_o),
                                                   per_oct(ap_o)],
        out_specs=pl.BlockSpec((lc, n_chunks, lb), lambda o, b: (0, b, o)),
        out_shape=jax.ShapeDtypeStruct((lc, rows // lc, w), F32),
        scratch_shapes=[pltpu.VMEM((lc * lb, lc * lb), BF16),
                        pltpu.VMEM((lc * lb, gpb * 2 * n), BF16),
                        pltpu.VMEM((gpb * 2 * n, lc * lb), BF16),
                        pltpu.VMEM((ap_rows, gpb * 2 * n), F32)],
        compiler_params=_cparams("parallel", "arbitrary"),
        name="s5_scan",
    )(*([u2] * lc), m0_o, e_o, p_o, ap_o)
    return jnp.swapaxes(ys, 0, 1).reshape(rows, w)


(_V_W0, _V_A0, _V_V0, _V_KK, _V_KA, _V_RK, _V_LNW, _V_LNB, _V_MUR, _V_MUK, _V_MUV) = range(11)
_VEC_ROWS = 16


def _dot_nt(a, b, **kw):
    return lax.dot_general(a, b, (((1,), (1,)), ((), ())), preferred_element_type=F32, **kw)


def _dot_tn(a, b, **kw):
    return lax.dot_general(a, b, (((0,), (0,)), ((), ())), preferred_element_type=F32, **kw)


def _mm(a, b):
    return jnp.dot(a.astype(BF16), b.astype(BF16), preferred_element_type=F32)


def _rwkv_solve(pairs):
    lc, pw = pairs[0][0].shape
    hd = RWKV_HEAD
    assert lc == hd and pw == 2 * hd
    lane = lax.broadcasted_iota(jnp.int32, (lc, pw), 1)
    row = lax.broadcasted_iota(jnp.int32, (lc, pw), 0)
    first = lane < hd
    col = jnp.where(first, lane, lane - hd)
    strict2 = row > col
    incl2 = row >= col
    ak_mask = jnp.logical_and(strict2, jnp.logical_not(first))
    zeros = jnp.zeros((lc, pw), F32)
    heads = [(p, e) for p in range(len(pairs)) for e in range(2)]

    gms = []
    for rt, at, bt, kt, v in pairs:
        ar = jnp.concatenate([jnp.where(first, at, 0.0), jnp.where(first, rt, 0.0),
                              jnp.where(first, 0.0, at), jnp.where(first, 0.0, rt)], axis=0)
        gms.append(_dot_nt(ar.astype(BF16), jnp.concatenate([bt, kt], axis=0).astype(BF16)))
    ga = [gms[p][2 * e * lc:(2 * e + 1) * lc] for p, e in heads]
    gr = [gms[p][(2 * e + 1) * lc:(2 * e + 2) * lc] for p, e in heads]
    vr = [pltpu.roll(pr[4], hd, axis=1) for pr in pairs]
    vz = [jnp.concatenate([zeros, r], axis=0).astype(BF16) for r in vr]

    nk = [jnp.where(strict2[:, :lc], g[:, :lc], 0.0).astype(BF16) for g in ga]
    y = []
    for i, (p, e) in enumerate(heads):
        akv = _mm(jnp.where(ak_mask, ga[i], 0.0), vz[p])
        y.append(jnp.where(first, pairs[p][1], akv) if e == 0 else jnp.where(first, akv, pairs[p][1]))
    steps = max(1, (lc - 1).bit_length())
    for i in range(steps):
        y = [yy + _mm(nn, yy) for nn, yy in zip(nk, y)]
        if i + 1 < steps:
            nk = [_mm(nn, nn).astype(BF16) for nn in nk]
    rz = []
    for i, (p, e) in enumerate(heads):
        zv = jnp.where(first, 0.0, vr[p]) if e == 0 else jnp.where(first, vr[p], 0.0)
        rz.append(_mm(jnp.where(incl2, gr[i], 0.0), jnp.concatenate([y[i], zv], axis=0)))

    outs = []
    for p, (rt, at, bt, kt, v) in enumerate(pairs):
        y0, y1, rz0, rz1 = y[2 * p], y[2 * p + 1], rz[2 * p], rz[2 * p + 1]
        outs.append((jnp.where(first, y0, y1),
                     rt + jnp.where(first, rz0, rz1),
                     pltpu.roll(jnp.where(first, y1, y0), hd, axis=1),
                     pltpu.roll(jnp.where(first, rz1, rz0), hd, axis=1)))
    return outs


def _rwkv_state(solved, tails, states):
    lc, pw = solved[0][0].shape
    hd = RWKV_HEAD
    uos = [_dot_nt(jnp.concatenate([x1, z1], axis=0).astype(BF16), s0.astype(BF16))
           for (x1, z1, x2, z2), s0 in zip(solved, states)]
    si = lax.broadcasted_iota(jnp.int32, (pw, pw), 0)
    sj = lax.broadcasted_iota(jnp.int32, (pw, pw), 1)
    same_head = (si < hd) == (sj < hd)
    outs = []
    for uo, (x1, z1, x2, z2), (v, bh, kh, wl), s0 in zip(uos, solved, tails, states):
        u = uo[:lc] + x2
        o = uo[lc:] + z2
        upd = _dot_tn(jnp.concatenate([u, v], axis=0).astype(BF16),
                      jnp.concatenate([bh, kh], axis=0).astype(BF16))
        outs.append((o, s0 * wl + jnp.where(same_head, upd, 0.0)))
    return outs


def _rwkv_kernel(*refs, has_vres, n_heads):
    if has_vres:
        (r_ref, k_ref, v_ref, l_ref, vf_ref, vec_ref, mul_ref, w2_ref, a2_ref, g2_ref, v2_ref,
         o_ref, s_sc, pr_sc, pk_sc, pv_sc, pl_sc, big_sc, on_sc, bonus_sc, gate_sc) = refs
        vout_ref = None
    else:
        (r_ref, k_ref, v_ref, l_ref, vec_ref, mul_ref, w2_ref, a2_ref, g2_ref,
         o_ref, vout_ref, s_sc, pr_sc, pk_sc, pv_sc, pl_sc, big_sc, on_sc, bonus_sc, gate_sc) = refs
    nr, width = r_ref.shape
    lc = RWKV_CHUNK
    n_sub = nr // lc
    hd = RWKV_HEAD
    pw = 2 * hd

    @pl.when(pl.program_id(1) == 0)
    def _():
        s_sc[...] = jnp.zeros_like(s_sc)
        pr_sc[...] = jnp.zeros_like(pr_sc)
        pk_sc[...] = jnp.zeros_like(pk_sc)
        pv_sc[...] = jnp.zeros_like(pv_sc)
        pl_sc[...] = jnp.zeros_like(pl_sc)

    def vec(i):
        return vec_ref[i:i + 1, :]

    n_grp = RWKV_GROUPS if n_sub % RWKV_GROUPS == 0 else 1
    cpg = n_sub // n_grp
    g_rows = cpg * lc

    def prologue(g, prevs):
        rows = slice(g * g_rows, (g + 1) * g_rows)
        row = lax.broadcasted_iota(jnp.int32, (g_rows, 1), 0)

        def shift_mix(x_ref, prev, mu):
            x = x_ref[rows, :]
            before = jnp.where(row == 0, prev, pltpu.roll(x, 1, axis=0))
            return x + mu * (before - x), x[g_rows - 1:g_rows, :]

        xr, last_r = shift_mix(r_ref, prevs[0], vec(_V_MUR))
        xk, last_k = shift_mix(k_ref, prevs[1], vec(_V_MUK))
        xv, last_v = shift_mix(v_ref, prevs[2], vec(_V_MUV))
        xl, last_l = shift_mix(l_ref, prevs[3], mul_ref[...])
        xwa, xg, xvr = xl[:, 0:128], xl[:, 128:384], xl[:, 384:512]

        logw = (-math.exp(-0.5)) * jax.nn.sigmoid(vec(_V_W0) + _mm(jnp.tanh(xwa), w2_ref[...]))
        iclr = jax.nn.sigmoid(vec(_V_A0) + _mm(xwa, a2_ref[...]))
        gate_sc[rows, :] = _mm(jax.nn.sigmoid(xg), g2_ref[...])
        if has_vres:
            v = xv + (vf_ref[rows, :] - xv) * jax.nn.sigmoid(vec(_V_V0) + _mm(xvr, v2_ref[...]))
        else:
            v = xv
            vout_ref[rows, :] = xv
        k = xk * (1.0 + (iclr - 1.0) * vec(_V_KA))

        ti = lax.broadcasted_iota(jnp.int32, (g_rows, g_rows), 0)
        tj = lax.broadcasted_iota(jnp.int32, (g_rows, g_rows), 1)
        tri = jnp.logical_and(ti >= tj, ti // lc == tj // lc)
        logw_hi = logw.astype(BF16)
        logw_lo = (logw - logw_hi.astype(F32)).astype(BF16)
        tri = tri.astype(BF16)
        cum = (jnp.dot(tri, logw_hi, preferred_element_type=F32)
               + jnp.dot(tri, logw_lo, preferred_element_type=F32))
        cum_last = cum[lc - 1:lc, :]
        for h in range(1, cpg):
            cum_last = jnp.where(row // lc == h, cum[(h + 1) * lc - 1:(h + 1) * lc, :], cum_last)
        dec_inv = jnp.exp(-cum)
        dec_tail = jnp.exp(cum_last - cum)

        big_sc[0, rows, :] = xr * jnp.exp(cum)
        big_sc[1, rows, :] = xk * vec(_V_KK)
        big_sc[2, rows, :] = iclr
        big_sc[3, rows, :] = k * dec_inv
        big_sc[4, rows, :] = v
        big_sc[5, rows, :] = jnp.exp(cum - logw)
        big_sc[6, rows, :] = k * dec_tail
        big_sc[7, rows, :] = xr * k * vec(_V_RK)
        big_sc[8, rows, :] = dec_inv
        big_sc[9, rows, :] = dec_tail
        big_sc[10, rows, :] = jnp.broadcast_to(jnp.exp(cum_last), (g_rows, width))
        return last_r, last_k, last_v, last_l

    prevs = (pr_sc[...], pk_sc[...], pv_sc[...], pl_sc[...])
    for g in range(n_grp):
        prevs = prologue(g, prevs)
    pr_sc[...], pk_sc[...], pv_sc[...], pl_sc[...] = prevs

    first = lax.broadcasted_iota(jnp.int32, (lc, pw), 1) < hd

    def head_sum(x):
        s0 = jnp.sum(jnp.where(first, x, 0.0), axis=-1, keepdims=True)
        s1 = jnp.sum(jnp.where(first, 0.0, x), axis=-1, keepdims=True)
        return jnp.where(first, s0, s1)

    n_pairs = n_heads // 2
    states = [s_sc[hp] for hp in range(n_pairs)]
    for g in range(n_grp):
        chunks = range(g * cpg, (g + 1) * cpg)
        ins, tails = [], []
        for h in chunks:
            for hp in range(n_pairs):
                rows, lanes = slice(h * lc, (h + 1) * lc), slice(hp * pw, (hp + 1) * pw)
                rt, kkraw, iclr_p, kt, v_p, dex, kh, rk, dinv, dtail, wl = [
                    big_sc[i, rows, lanes] for i in range(11)]
                kkn = kkraw * lax.rsqrt(head_sum(kkraw * kkraw) + 1e-12)
                b = kkn * iclr_p
                ins.append((rt, -kkn * dex, b * dinv, kt, v_p))
                tails.append((v_p, b * dtail, kh, wl[0:1]))
                bonus_sc[rows, lanes] = head_sum(rk) * v_p
        solved = _rwkv_solve(ins)
        for j, h in enumerate(chunks):
            sel = slice(j * n_pairs, (j + 1) * n_pairs)
            outs = _rwkv_state(solved[sel], tails[sel], states)
            states = [s_new for _, s_new in outs]
            for hp, (o, _) in enumerate(outs):
                cen = o - head_sum(o) * (1.0 / hd)
                var = head_sum(cen * cen) * (1.0 / hd)
                on_sc[h * lc:(h + 1) * lc, hp * pw:(hp + 1) * pw] = cen * lax.rsqrt(var + LNX_EPS)
    for hp in range(n_pairs):
        s_sc[hp] = states[hp]
    o_ref[...] = (on_sc[...] * vec(_V_LNW) + vec(_V_LNB) + bonus_sc[...]) * gate_sc[...]


def _rwkv(p, vfirst, vecs, mu_l, w2, a2, g2p, v2p, *, batch, width, col0, lora_col):
    rows = p.shape[0]
    tp = rows // batch
    n_sub = RWKV_SUBCHUNKS if tp % (RWKV_SUBCHUNKS * RWKV_CHUNK) == 0 else 1
    lc = n_sub * RWKV_CHUNK
    nck = tp // lc
    n_heads = width // RWKV_HEAD
    has_vres = vfirst is not None
    rowblk = lambda b, c: b * nck + c
    cb = col0 // width
    tok = lambda j: pl.BlockSpec((lc, width), lambda b, c: (rowblk(b, c), j))
    full = lambda a: pl.BlockSpec(a.shape, lambda b, c: (0,) * a.ndim)
    in_specs = [tok(cb), tok(cb + 1), tok(cb + 2),
                pl.BlockSpec((lc, LORA_BLOCK), lambda b, c: (rowblk(b, c), lora_col // LORA_BLOCK))]
    args = [p, p, p, p]
    if has_vres:
        in_specs.append(tok(0))
        args.append(vfirst)
    in_specs += [full(vecs), full(mu_l), full(w2), full(a2), full(g2p)]
    args += [vecs, mu_l, w2, a2, g2p]
    if has_vres:
        in_specs.append(full(v2p))
        args.append(v2p)
    out_shape = [jax.ShapeDtypeStruct((rows, width), F32)]
    out_specs = [tok(0)]
    if not has_vres:
        out_shape.append(jax.ShapeDtypeStruct((rows, width), F32))
        out_specs.append(tok(0))
    outs = pl.pallas_call(
        functools.partial(_rwkv_kernel, has_vres=has_vres, n_heads=n_heads),
        grid=(batch, nck),
        in_specs=in_specs,
        out_specs=out_specs,
        out_shape=out_shape,
        scratch_shapes=[pltpu.VMEM((n_heads // 2, 2 * RWKV_HEAD, 2 * RWKV_HEAD), F32),
                        pltpu.VMEM((1, width), F32), pltpu.VMEM((1, width), F32),
                        pltpu.VMEM((1, width), F32), pltpu.VMEM((1, LORA_BLOCK), F32),
                        pltpu.VMEM((11, lc, width), F32), pltpu.VMEM((lc, width), F32),
                        pltpu.VMEM((lc, width), F32), pltpu.VMEM((lc, width), F32)],
        compiler_params=_cparams("parallel", "arbitrary"),
        name="rwkv7",
    )(*args)
    return outs if not has_vres else (outs[0], None)


def _merge_kernel(x_ref, ga_ref, gb_ref, u_ref, ys_ref, yb_ref, d_ref, wglu_ref, wua_ref, wub_ref,
                  wo_ref, o_ref):
    y = ys_ref[...] + d_ref[...] * u_ref[...]
    y = jax.nn.gelu(y, approximate=True)
    ya = y * jax.nn.sigmoid(jnp.dot(y.astype(BF16), wglu_ref[...], preferred_element_type=F32))
    up_a = jnp.dot(ya.astype(BF16), wua_ref[...], preferred_element_type=F32)
    up_b = jnp.dot(yb_ref[...].astype(BF16), wub_ref[...], preferred_element_type=F32)
    merged = jax.nn.sigmoid(ga_ref[...]) * up_a + jax.nn.sigmoid(gb_ref[...]) * up_b
    o_ref[...] = x_ref[...] + jnp.dot(merged.astype(BF16), wo_ref[...], preferred_element_type=F32)


def _merge(x, p, ys, yb, d_skip, wglu, wua, wub, wo, *, u_col, layer):
    rows, d = x.shape
    w = ys.shape[1]
    tm = MERGE_ROW_TILE if rows % MERGE_ROW_TILE == 0 else min(256, _row_tile(rows))
    row = lambda width, j: pl.BlockSpec((tm, width), lambda i: (i, j))
    const = lambda a: pl.BlockSpec((None,) + a.shape[1:], lambda i: (layer, 0, 0),
                                   pipeline_mode=pl.Buffered(1))
    return pl.pallas_call(
        _merge_kernel,
        grid=(rows // tm,),
        in_specs=[row(d, 0), row(d, 0), row(d, 1), row(w, u_col // w), row(w, 0), row(w, 0),
                  pl.BlockSpec(d_skip.shape, lambda i: (0, 0)),
                  const(wglu), const(wua), const(wub), const(wo)],
        out_specs=row(d, 0),
        out_shape=jax.ShapeDtypeStruct((rows, d), F32),
        compiler_params=_cparams("parallel"),
        name="merge",
    )(x, p, p, p, ys, yb, d_skip, wglu, wua, wub, wo)


def _norm_kernel(x_ref, g_ref, o_ref):
    o_ref[...] = _rms(x_ref[...], g_ref[...])


def _final_norm(x, g):
    rows, d = x.shape
    tm = _row_tile(rows)
    return pl.pallas_call(
        _norm_kernel,
        grid=(rows // tm,),
        in_specs=[pl.BlockSpec((tm, d), lambda i: (i, 0)), pl.BlockSpec((1, d), lambda i: (0, 0))],
        out_specs=pl.BlockSpec((tm, d), lambda i: (i, 0)),
        out_shape=jax.ShapeDtypeStruct((rows, d), F32),
        compiler_params=_cparams("parallel"),
        name="final_norm",
    )(x, g)


def _pad_cols(a, width):
    return jnp.pad(a, ((0, 0), (0, width - a.shape[1])))


def _pad_rows(a, height):
    return jnp.pad(a, ((0, height - a.shape[0]), (0, 0)))


def kernel(x, meta_tokens, ffn1_norm, ffn1_w_gate, ffn1_w_up, ffn1_w_down, mix_norm, w_in_first, w_in_rest, mu_shift, mu_vres, ssm_lambda_re, ssm_lambda_im, ssm_log_dt, ssm_b_re, ssm_b_im, ssm_c_re, ssm_c_im, ssm_d, ssm_w_glu, rwkv_w0, rwkv_w2, rwkv_a0, rwkv_a2, rwkv_v0, rwkv_v2, rwkv_g2, rwkv_k_k, rwkv_k_a, rwkv_r_k, rwkv_lnx_w, rwkv_lnx_b, w_up_ssm, w_up_rwkv, w_out, ffn2_norm, ffn2_w_gate, ffn2_w_up, ffn2_w_down, final_norm):
    bsz, seq, d = x.shape
    n_meta = meta_tokens.shape[0]
    depth = ffn1_norm.shape[0]
    w = ssm_d.shape[1]
    n_groups = ssm_b_re.shape[1]
    ld, la, lg, lv = rwkv_w2.shape[1], rwkv_a2.shape[1], rwkv_g2.shape[1], rwkv_v2.shape[1]
    assert w == rwkv_w0.shape[1] and 2 * w == d and w % 128 == 0
    assert n_groups * SSM_GROUP == w and ld + la <= 128 and lg <= 256 and lv <= 128
    t_len = n_meta + seq
    tp = -(-t_len // SEQ_ALIGN) * SEQ_ALIGN
    rows = bsz * tp

    meta = jnp.broadcast_to(meta_tokens[None].astype(x.dtype), (bsz, n_meta, d))
    h_res = jnp.concatenate([meta, x, jnp.zeros((bsz, tp - t_len, d), x.dtype)], axis=1)
    h_res = h_res.reshape(rows, d)

    u_col = 2 * d
    rkv_col = u_col + w
    lora_col = rkv_col + 3 * w
    c_w = 4 * w
    c_g = c_w + ld + la
    c_ga = c_g + lg
    c_gb = c_ga + d
    p_common = c_gb + d

    ffn1_w = [a.astype(BF16) for a in (ffn1_w_gate, ffn1_w_up, ffn1_w_down)]
    ffn2_w = [a.astype(BF16) for a in (ffn2_w_gate, ffn2_w_up, ffn2_w_down)]
    merge_w = [a.astype(BF16) for a in (ssm_w_glu, w_up_ssm, w_up_rwkv, w_out)]

    v_first = None
    for i in range(depth):
        h_res = _ffn(h_res, ffn1_norm[i][None], *ffn1_w, i)

        w_in = w_in_first if i == 0 else w_in_rest[i - 1]
        blocks = [w_in[:, c_ga:c_gb], w_in[:, c_gb:p_common], w_in[:, :c_w],
                  _pad_cols(w_in[:, c_w:c_g], 128), _pad_cols(w_in[:, c_g:c_ga], 256),
                  _pad_cols(w_in[:, p_common:], 128)]
        p, u = _proj(h_res, mix_norm[i][None], jnp.concatenate(blocks, axis=1).astype(BF16),
                     u_col, w)

        ys = _s5_scan(u, ssm_log_dt[i], ssm_lambda_re[i], ssm_lambda_im[i],
                      ssm_c_re[i], ssm_c_im[i], ssm_b_re[i], ssm_b_im[i], batch=bsz)

        mu = mu_shift[i]
        zero_w = jnp.zeros((w,), F32)
        vecs = jnp.stack([rwkv_w0[i], rwkv_a0[i], rwkv_v0[i - 1] if i else zero_w, rwkv_k_k[i],
                          rwkv_k_a[i], rwkv_r_k[i].reshape(w), rwkv_lnx_w[i], rwkv_lnx_b[i],
                          mu[0:w], mu[w:2 * w], mu[2 * w:3 * w]])
        vecs = _pad_rows(vecs, _VEC_ROWS)
        mu_v = mu_vres[i - 1] if i else jnp.zeros((lv,), F32)
        mu_l = jnp.concatenate([jnp.pad(mu[3 * w:3 * w + ld + la], (0, 128 - ld - la)),
                                jnp.pad(mu[3 * w + ld + la:], (0, 256 - lg)),
                                jnp.pad(mu_v, (0, 128 - lv))])[None]
        w2p = _pad_rows(rwkv_w2[i], 128).astype(BF16)
        a2p = jnp.pad(rwkv_a2[i], ((ld, 128 - ld - la), (0, 0))).astype(BF16)
        g2p = _pad_rows(rwkv_g2[i], 256).astype(BF16)
        v2p = _pad_rows(rwkv_v2[i - 1], 128).astype(BF16) if i else None
        yb, v_new = _rwkv(p, v_first, vecs, mu_l, w2p, a2p, g2p, v2p,
                          batch=bsz, width=w, col0=rkv_col, lora_col=lora_col)
        if i == 0:
            v_first = v_new

        h_res = _merge(h_res, p, ys, yb, ssm_d[i][None], *merge_w, u_col=u_col, layer=i)

        h_res = _ffn(h_res, ffn2_norm[i][None], *ffn2_w, i)

    out = _final_norm(h_res, final_norm[None]).reshape(bsz, tp, d)
    return out[:, n_meta:t_len]
```

```python
import functools
import math

import jax
import jax.numpy as jnp
from jax import lax
from jax.experimental import pallas as pl
from jax.experimental.pallas import tpu as pltpu

F32 = jnp.float32
BF16 = jnp.bfloat16
HIGHEST = lax.Precision.HIGHEST

NORM_EPS = 1e-6
RWKV_HEAD = 64
LNX_EPS = RWKV_HEAD * 1e-5
SSM_GROUP = 16
S5_CHUNK = 16
RWKV_CHUNK = 64
RWKV_SUBCHUNKS = 3
RWKV_GROUPS = 1
FFN_ROW_TILE = 1056
MERGE_ROW_TILE = 384
PROJ_ROW_TILE = 1536
LORA_BLOCK = 512
ROW_ALIGN = 8
SEQ_ALIGN = 128
VMEM_LIMIT_BYTES = 56 * 1024 * 1024


def _cparams(*sem):
    return pltpu.CompilerParams(dimension_semantics=sem, vmem_limit_bytes=VMEM_LIMIT_BYTES)


def _rms(x, g):
    inv = lax.rsqrt(jnp.mean(x * x, axis=-1, keepdims=True) + NORM_EPS)
    return (x * inv) * g


def _row_tile(rows):
    for tm in (512, 256, 128, 64, 32, 16, 8):
        if rows % tm == 0:
            return tm
    raise ValueError(rows)


def _col_tile(cols):
    for tn in (512, 256, 128):
        if cols % tn == 0:
            return tn
    raise ValueError(cols)


def _ffn_kernel(x_ref, g_ref, wg_ref, wu_ref, wd_ref, o_ref, h_sc):
    f = pl.program_id(1)

    @pl.when(f == 0)
    def _():
        x = x_ref[...]
        h_sc[...] = _rms(x, g_ref[...]).astype(BF16)
        o_ref[...] = x

    h = h_sc[...]
    a = jnp.dot(h, wg_ref[...], preferred_element_type=F32)
    b = jnp.dot(h, wu_ref[...], preferred_element_type=F32)
    z = (0.5 * (a * jax.nn.sigmoid(a)) * b).astype(BF16)
    o_ref[...] += jnp.dot(z, wd_ref[...], preferred_element_type=F32)


def _ffn(x, g, wg, wu, wd, layer):
    rows, d = x.shape
    dff = wg.shape[2]
    tm = FFN_ROW_TILE if rows % FFN_ROW_TILE == 0 else _row_tile(rows)
    tf = _col_tile(dff)
    return pl.pallas_call(
        _ffn_kernel,
        grid=(rows // tm, dff // tf),
        in_specs=[
            pl.BlockSpec((tm, d), lambda i, f: (i, 0)),
            pl.BlockSpec((1, d), lambda i, f: (0, 0)),
            pl.BlockSpec((None, d, tf), lambda i, f: (layer, 0, f)),
            pl.BlockSpec((None, d, tf), lambda i, f: (layer, 0, f)),
            pl.BlockSpec((None, tf, d), lambda i, f: (layer, f, 0)),
        ],
        out_specs=pl.BlockSpec((tm, d), lambda i, f: (i, 0)),
        out_shape=jax.ShapeDtypeStruct((rows, d), F32),
        scratch_shapes=[pltpu.VMEM((tm, d), BF16)],
        compiler_params=_cparams("parallel", "arbitrary"),
        name="ffn",
    )(x, g, wg, wu, wd)


def _proj_kernel(x_ref, g_ref, w_ref, o_ref, h_sc):
    @pl.when(pl.program_id(1) == 0)
    def _():
        h_sc[...] = _rms(x_ref[...], g_ref[...]).astype(BF16)

    o_ref[...] = jnp.dot(h_sc[...], w_ref[...], preferred_element_type=F32)


def _proj_split_kernel(x_ref, g_ref, w_ref, o_ref, u_ref, h_sc, *, j0, nu):
    _proj_kernel(x_ref, g_ref, w_ref, o_ref, h_sc)
    j = pl.program_id(1)

    @pl.when(jnp.logical_and(j >= j0, j < j0 + nu))
    def _():
        u_ref[...] = o_ref[...]


def _proj(x, g, w, layer, u_col, u_width):
    rows, d = x.shape
    n = w.shape[2]
    tm = PROJ_ROW_TILE if rows % PROJ_ROW_TILE == 0 else _row_tile(rows)
    tn = _col_tile(n)
    split = u_col % tn == 0 and u_width % tn == 0
    in_specs = [
        pl.BlockSpec((tm, d), lambda i, j: (i, 0)),
        pl.BlockSpec((1, d), lambda i, j: (0, 0)),
        pl.BlockSpec((None, d, tn), lambda i, j: (layer, 0, j)),
    ]
    p_spec = pl.BlockSpec((tm, tn), lambda i, j: (i, j))
    p_shape = jax.ShapeDtypeStruct((rows, n), F32)
    common = dict(grid=(rows // tm, n // tn), in_specs=in_specs,
                  scratch_shapes=[pltpu.VMEM((tm, d), BF16)],
                  compiler_params=_cparams("parallel", "arbitrary"), name="in_proj")
    if not split:
        p = pl.pallas_call(_proj_kernel, out_specs=p_spec, out_shape=p_shape, **common)(x, g, w)
        return p, p[:, u_col:u_col + u_width]
    j0, nu = u_col // tn, u_width // tn
    u_spec = pl.BlockSpec((tm, tn), lambda i, j: (i, jnp.clip(j - j0, 0, nu - 1)))
    return pl.pallas_call(
        functools.partial(_proj_split_kernel, j0=j0, nu=nu),
        out_specs=[p_spec, u_spec],
        out_shape=[p_shape, jax.ShapeDtypeStruct((rows, u_width), F32)],
        **common)(x, g, w)


def _cmul(ar, ai, br, bi):
    return ar * br - ai * bi, ar * bi + ai * br


def _cpow_int(a_re, a_im, expo, n_bits):
    shape = jnp.broadcast_shapes(a_re.shape, expo.shape)
    p_re, p_im = jnp.ones(shape, F32), jnp.zeros(shape, F32)
    for bit in range(n_bits):
        take = jnp.bitwise_and(lax.shift_right_logical(expo, bit), 1) == 1
        q_re, q_im = _cmul(p_re, p_im, a_re, a_im)
        p_re, p_im = jnp.where(take, q_re, p_re), jnp.where(take, q_im, p_im)
        if bit + 1 < n_bits:
            a_re, a_im = _cmul(a_re, a_im, a_re, a_im)
    return p_re, p_im


def _s5_param_kernel(ldt_ref, lrc_ref, lic_ref, lrr_ref, lir_ref, ctre_ref, ctim_ref,
                     btre_ref, btim_ref, bttre_ref, bttim_ref, m0_ref, e_ref, p_ref, ap_ref,
                     *, n_chunks):
    lc = S5_CHUNK
    cw = SSM_GROUP
    width = lc * cw
    n_bits = (lc - 1).bit_length()
    dt = jnp.exp(ldt_ref[...])

    def abar(lr, li):
        mag = jnp.exp(lr * dt)
        return mag * jnp.cos(li * dt), mag * jnp.sin(li * dt)

    lr_r, li_r = lrr_ref[...], lir_ref[...]
    ac_re, ac_im = abar(lrc_ref[...], lic_ref[...])
    ar_re, ar_im = abar(lr_r, li_r)

    den = lr_r * lr_r + li_r * li_r
    nr, ni = ar_re - 1.0, ar_im
    q_re = (nr * lr_r + ni * li_r) / den
    q_im = (ni * lr_r - nr * li_r) / den

    lag = lax.broadcasted_iota(jnp.int32, (1, width), 1) // cw
    ct_re, ct_im = ctre_ref[...], ctim_ref[...]
    p_re, p_im = _cpow_int(ac_re, ac_im, lag, n_bits)
    d0_re, d0_im = _cmul(ct_re, ct_im, p_re, p_im)
    bt_re, bt_im = btre_ref[...], btim_ref[...]
    bbt_re, bbt_im = _cmul(q_re, q_im, bt_re, bt_im)
    m0_ref[...] = (jnp.dot(bbt_re, d0_re, preferred_element_type=F32, precision=HIGHEST)
                   - jnp.dot(bbt_im, d0_im, preferred_element_type=F32, precision=HIGHEST))

    rowj = lax.broadcasted_iota(jnp.int32, (width, 1), 0) // cw
    e_pre, e_pim = _cpow_int(ar_re, ar_im, (lc - 1) - rowj, n_bits)
    btt_re, btt_im = bttre_ref[...], bttim_ref[...]
    bb2_re, bb2_im = _cmul(q_re, q_im, btt_re, btt_im)
    e_re, e_im = _cmul(bb2_re, bb2_im, e_pre, e_pim)
    e_ref[...] = jnp.concatenate([e_re, e_im], axis=1)

    p1_re, p1_im = _cmul(p_re, p_im, ac_re, ac_im)
    d1_re, d1_im = _cmul(ct_re, ct_im, p1_re, p1_im)
    p_ref[...] = jnp.concatenate([d1_re, -d1_im], axis=0)

    ap_ref[...] = jnp.zeros_like(ap_ref)
    pre, pim = ar_re, ar_im
    for _ in range(n_bits):
        pre, pim = _cmul(pre, pim, pre, pim)
    assert 1 << n_bits == lc
    for i, _ in enumerate(_scan_shifts(n_chunks)):
        ap_ref[2 * i:2 * i + 1, :] = jnp.concatenate([pre, pre], axis=1)
        ap_ref[2 * i + 1:2 * i + 2, :] = jnp.concatenate([-pim, pim], axis=1)
        pre, pim = _cmul(pre, pim, pre, pim)


def _scan_shifts(n_chunks):
    shifts, s = [], 1
    while s < n_chunks:
        shifts.append(s)
        s *= 2
    return shifts


def _s5_main_kernel(*refs, n_chunks, n_state):
    lc = S5_CHUNK
    cw = SSM_GROUP
    x_refs = refs[:lc]
    m0_ref, e_ref, p_ref, ap_ref, y_ref, m_sc, e_sc, p_sc, ap_sc = refs[lc:]
    lb = x_refs[0].shape[1]
    gpb = lb // cw
    width = lc * cw
    sw = 2 * n_state

    @pl.when(pl.program_id(1) == 0)
    def _():
        def iota(shape, dim):
            return lax.broadcasted_iota(jnp.int32, shape, dim)

        def spread(shape, dim_small):
            small, big = iota(shape, dim_small), iota(shape, 1 - dim_small)
            return jnp.logical_and(small // cw == big // lb, small % cw == big % cw)

        lane_expand = spread((width, lc * lb), 0).astype(BF16)
        row_expand = spread((lc * lb, width), 1).astype(BF16)

        m0w = jnp.dot(m0_ref[...].astype(BF16), lane_expand, preferred_element_type=F32)
        own = iota(m0w.shape, 0) // cw == (iota(m0w.shape, 1) % lb) // cw
        m0 = jnp.where(own, m0w, 0.0).astype(BF16)
        for t in range(lc):
            blk = m0 if t == 0 else jnp.concatenate(
                [jnp.zeros((lb, t * lb), m0.dtype), m0[:, :(lc - t) * lb]], axis=1)
            m_sc[t * lb:(t + 1) * lb, :] = blk

        e_stack = jnp.concatenate([e_ref[g] for g in range(gpb)], axis=1).astype(BF16)
        e_full = jnp.dot(row_expand, e_stack, preferred_element_type=F32)
        own = (iota(e_full.shape, 0) % lb) // cw == iota(e_full.shape, 1) // sw
        e_sc[...] = jnp.where(own, e_full, 0.0).astype(BF16)

        p_stack = p_ref[...].reshape(gpb * sw, width).astype(BF16)
        p_full = jnp.dot(p_stack, lane_expand, preferred_element_type=F32)
        own = iota(p_full.shape, 0) // sw == (iota(p_full.shape, 1) % lb) // cw
        p_sc[...] = jnp.where(own, p_full, 0.0).astype(BF16)

        ap_sc[...] = jnp.concatenate([ap_ref[g] for g in range(gpb)], axis=1)

    x = jnp.concatenate([r[...].astype(BF16) for r in x_refs], axis=1)
    y = jnp.concatenate(
        [jnp.dot(x[:, :(q + 2) * lb], m_sc[:(q + 2) * lb, q * lb:(q + 2) * lb],
                 preferred_element_type=F32) for q in range(0, lc, 2)], axis=1)
    s = jnp.dot(x, e_sc[...], preferred_element_type=F32)

    def swap_re_im(a):
        return jnp.concatenate([pltpu.roll(a[:, j * sw:(j + 1) * sw], n_state, axis=1)
                                for j in range(a.shape[1] // sw)], axis=1)

    crow = lax.broadcasted_iota(jnp.int32, (n_chunks, 1), 0)
    for i, shift in enumerate(_scan_shifts(n_chunks)):
        prev = jnp.where(crow >= shift, pltpu.roll(s, shift, axis=0), 0.0)
        s = s + ap_sc[2 * i:2 * i + 1, :] * prev + ap_sc[2 * i + 1:2 * i + 2, :] * swap_re_im(prev)
    s_start = jnp.where(crow >= 1, pltpu.roll(s, 1, axis=0), 0.0)
    y = y + jnp.dot(s_start.astype(BF16), p_sc[...], preferred_element_type=F32)
    for t in range(lc):
        y_ref[t] = y[:, t * lb:(t + 1) * lb]


def _s5_scan(u, ldt, lr, li, c_re, c_im, b_re, b_im, *, batch):
    rows, w = u.shape
    g, n = lr.shape
    cw = SSM_GROUP
    lc = S5_CHUNK
    width = lc * cw
    lb = 128
    gpb = lb // cw
    n_oct = w // lb
    n_chunks = rows // batch // lc
    ap_rows = -(-2 * len(_scan_shifts(n_chunks)) // 8) * 8
    ct_re = jnp.tile(jnp.swapaxes(c_re, 1, 2), (1, 1, lc))
    ct_im = jnp.tile(jnp.swapaxes(c_im, 1, 2), (1, 1, lc))
    bt_re = jnp.swapaxes(b_re, 1, 2)
    bt_im = jnp.swapaxes(b_im, 1, 2)
    btt_re = jnp.tile(bt_re, (1, lc, 1))
    btt_im = jnp.tile(bt_im, (1, lc, 1))
    g3 = lambda a, b: pl.BlockSpec((None, a, b), lambda i: (i, 0, 0))
    m0, e_g, p_g, ap = pl.pallas_call(
        functools.partial(_s5_param_kernel, n_chunks=n_chunks),
        grid=(g,),
        in_specs=[g3(1, 1), g3(n, 1), g3(n, 1), g3(1, n), g3(1, n),
                  g3(n, width), g3(n, width), g3(cw, n), g3(cw, n), g3(width, n), g3(width, n)],
        out_specs=[g3(cw, width), g3(width, 2 * n), g3(2 * n, width), g3(ap_rows, 2 * n)],
        out_shape=[jax.ShapeDtypeStruct((g, cw, width), F32),
                   jax.ShapeDtypeStruct((g, width, 2 * n), F32),
                   jax.ShapeDtypeStruct((g, 2 * n, width), F32),
                   jax.ShapeDtypeStruct((g, ap_rows, 2 * n), F32)],
        compiler_params=_cparams("parallel"),
        name="s5_params",
    )(ldt.reshape(g, 1, 1), lr.reshape(g, n, 1), li.reshape(g, n, 1), lr.reshape(g, 1, n),
      li.reshape(g, 1, n), ct_re, ct_im, bt_re, bt_im, btt_re, btt_im)

    m0_o = m0.reshape(n_oct, lb, width)
    e_o = e_g.reshape(n_oct, gpb, width, 2 * n)
    p_o = p_g.reshape(n_oct, gpb, 2 * n, width)
    ap_o = ap.reshape(n_oct, gpb, ap_rows, 2 * n)

    u2 = u.reshape(rows // lc, lc * w)
    xspec = lambda t: pl.BlockSpec((n_chunks, lb), lambda o, b: (b, t * n_oct + o))
    per_oct = lambda a: pl.BlockSpec((None,) + a.shape[1:], lambda o, b: (o,) + (0,) * (a.ndim - 1))
    ys = pl.pallas_call(
        functools.partial(_s5_main_kernel, n_chunks=n_chunks, n_state=n),
        grid=(n_oct, batch),
        in_specs=[xspec(t) for t in range(lc)] + [per_oct(m0_o), per_oct(e_o), per_oct(p_o),
                                                   per_oct(ap_o)],
        out_specs=pl.BlockSpec((lc, n_chunks, lb), lambda o, b: (0, b, o)),
        out_shape=jax.ShapeDtypeStruct((lc, rows // lc, w), F32),
        scratch_shapes=[pltpu.VMEM((lc * lb, lc * lb), BF16),
                        pltpu.VMEM((lc * lb, gpb * 2 * n), BF16),
                        pltpu.VMEM((gpb * 2 * n, lc * lb), BF16),
                        pltpu.VMEM((ap_rows, gpb * 2 * n), F32)],
        compiler_params=_cparams("parallel", "arbitrary"),
        name="s5_scan",
    )(*([u2] * lc), m0_o, e_o, p_o, ap_o)
    return jnp.swapaxes(ys, 0, 1).reshape(rows, w)


(_V_W0, _V_A0, _V_V0, _V_KK, _V_KA, _V_RK, _V_LNW, _V_LNB, _V_MUR, _V_MUK, _V_MUV) = range(11)
_VEC_ROWS = 16


def _dot_nt(a, b, **kw):
    return lax.dot_general(a, b, (((1,), (1,)), ((), ())), preferred_element_type=F32, **kw)


def _dot_tn(a, b, **kw):
    return lax.dot_general(a, b, (((0,), (0,)), ((), ())), preferred_element_type=F32, **kw)


def _mm(a, b):
    return jnp.dot(a.astype(BF16), b.astype(BF16), preferred_element_type=F32)


def _rwkv_solve(pairs):
    lc, pw = pairs[0][0].shape
    hd = RWKV_HEAD
    assert lc == hd and pw == 2 * hd
    lane = lax.broadcasted_iota(jnp.int32, (lc, pw), 1)
    row = lax.broadcasted_iota(jnp.int32, (lc, pw), 0)
    first = lane < hd
    col = jnp.where(first, lane, lane - hd)
    strict2 = row > col
    incl2 = row >= col
    ak_mask = jnp.logical_and(strict2, jnp.logical_not(first))
    zeros = jnp.zeros((lc, pw), F32)
    heads = [(p, e) for p in range(len(pairs)) for e in range(2)]

    gms = []
    for rt, at, bt, kt, v in pairs:
        ar = jnp.concatenate([jnp.where(first, at, 0.0), jnp.where(first, rt, 0.0),
                              jnp.where(first, 0.0, at), jnp.where(first, 0.0, rt)], axis=0)
        gms.append(_dot_nt(ar.astype(BF16), jnp.concatenate([bt, kt], axis=0).astype(BF16)))
    ga = [gms[p][2 * e * lc:(2 * e + 1) * lc] for p, e in heads]
    gr = [gms[p][(2 * e + 1) * lc:(2 * e + 2) * lc] for p, e in heads]
    vr = [pltpu.roll(pr[4], hd, axis=1) for pr in pairs]
    vz = [jnp.concatenate([zeros, r], axis=0).astype(BF16) for r in vr]

    nk = [jnp.where(strict2[:, :lc], g[:, :lc], 0.0).astype(BF16) for g in ga]
    y = []
    for i, (p, e) in enumerate(heads):
        akv = _mm(jnp.where(ak_mask, ga[i], 0.0), vz[p])
        y.append(jnp.where(first, pairs[p][1], akv) if e == 0 else jnp.where(first, akv, pairs[p][1]))
    steps = max(1, (lc - 1).bit_length())
    for i in range(steps):
        y = [yy + _mm(nn, yy) for nn, yy in zip(nk, y)]
        if i + 1 < steps:
            nk = [_mm(nn, nn).astype(BF16) for nn in nk]
    rz = []
    for i, (p, e) in enumerate(heads):
        zv = jnp.where(first, 0.0, vr[p]) if e == 0 else jnp.where(first, vr[p], 0.0)
        rz.append(_mm(jnp.where(incl2, gr[i], 0.0), jnp.concatenate([y[i], zv], axis=0)))

    outs = []
    for p, (rt, at, bt, kt, v) in enumerate(pairs):
        y0, y1, rz0, rz1 = y[2 * p], y[2 * p + 1], rz[2 * p], rz[2 * p + 1]
        outs.append((jnp.where(first, y0, y1),
                     rt + jnp.where(first, rz0, rz1),
                     pltpu.roll(jnp.where(first, y1, y0), hd, axis=1),
                     pltpu.roll(jnp.where(first, rz1, rz0), hd, axis=1)))
    return outs


def _rwkv_state(solved, tails, states):
    lc, pw = solved[0][0].shape
    hd = RWKV_HEAD
    uos = [_dot_nt(jnp.concatenate([x1, z1], axis=0).astype(BF16), s0.astype(BF16))
           for (x1, z1, x2, z2), s0 in zip(solved, states)]
    si = lax.broadcasted_iota(jnp.int32, (pw, pw), 0)
    sj = lax.broadcasted_iota(jnp.int32, (pw, pw), 1)
    same_head = (si < hd) == (sj < hd)
    outs = []
    for uo, (x1, z1, x2, z2), (v, bh, kh, wl), s0 in zip(uos, solved, tails, states):
        u = uo[:lc] + x2
        o = uo[lc:] + z2
        upd = _dot_tn(jnp.concatenate([u, v], axis=0).astype(BF16),
                      jnp.concatenate([bh, kh], axis=0).astype(BF16))
        outs.append((o, s0 * wl + jnp.where(same_head, upd, 0.0)))
    return outs


def _rwkv_kernel(*refs, has_vres, n_heads):
    if has_vres:
        (r_ref, k_ref, v_ref, l_ref, vf_ref, vec_ref, mul_ref, w2_ref, a2_ref, g2_ref, v2_ref,
         o_ref, s_sc, pr_sc, pk_sc, pv_sc, pl_sc, big_sc, on_sc, bonus_sc, gate_sc) = refs
        vout_ref = None
    else:
        (r_ref, k_ref, v_ref, l_ref, vec_ref, mul_ref, w2_ref, a2_ref, g2_ref,
         o_ref, vout_ref, s_sc, pr_sc, pk_sc, pv_sc, pl_sc, big_sc, on_sc, bonus_sc, gate_sc) = refs
    nr, width = r_ref.shape
    lc = RWKV_CHUNK
    n_sub = nr // lc
    hd = RWKV_HEAD
    pw = 2 * hd

    @pl.when(pl.program_id(1) == 0)
    def _():
        s_sc[...] = jnp.zeros_like(s_sc)
        pr_sc[...] = jnp.zeros_like(pr_sc)
        pk_sc[...] = jnp.zeros_like(pk_sc)
        pv_sc[...] = jnp.zeros_like(pv_sc)
        pl_sc[...] = jnp.zeros_like(pl_sc)

    def vec(i):
        return vec_ref[i:i + 1, :]

    n_grp = RWKV_GROUPS if n_sub % RWKV_GROUPS == 0 else 1
    cpg = n_sub // n_grp
    g_rows = cpg * lc

    def prologue(g, prevs):
        rows = slice(g * g_rows, (g + 1) * g_rows)
        row = lax.broadcasted_iota(jnp.int32, (g_rows, 1), 0)

        def shift_mix(x_ref, prev, mu):
            x = x_ref[rows, :]
            before = jnp.where(row == 0, prev, pltpu.roll(x, 1, axis=0))
            return x + mu * (before - x), x[g_rows - 1:g_rows, :]

        xr, last_r = shift_mix(r_ref, prevs[0], vec(_V_MUR))
        xk, last_k = shift_mix(k_ref, prevs[1], vec(_V_MUK))
        xv, last_v = shift_mix(v_ref, prevs[2], vec(_V_MUV))
        xl, last_l = shift_mix(l_ref, prevs[3], mul_ref[...])
        xwa, xg, xvr = xl[:, 0:128], xl[:, 128:384], xl[:, 384:512]

        logw = (-math.exp(-0.5)) * jax.nn.sigmoid(vec(_V_W0) + _mm(jnp.tanh(xwa), w2_ref[...]))
        iclr = jax.nn.sigmoid(vec(_V_A0) + _mm(xwa, a2_ref[...]))
        gate_sc[rows, :] = _mm(jax.nn.sigmoid(xg), g2_ref[...])
        if has_vres:
            v = xv + (vf_ref[rows, :] - xv) * jax.nn.sigmoid(vec(_V_V0) + _mm(xvr, v2_ref[...]))
        else:
            v = xv
            vout_ref[rows, :] = xv
        k = xk * (1.0 + (iclr - 1.0) * vec(_V_KA))

        ti = lax.broadcasted_iota(jnp.int32, (g_rows, g_rows), 0)
        tj = lax.broadcasted_iota(jnp.int32, (g_rows, g_rows), 1)
        tri = jnp.logical_and(ti >= tj, ti // lc == tj // lc)
        logw_hi = logw.astype(BF16)
        logw_lo = (logw - logw_hi.astype(F32)).astype(BF16)
        tri = tri.astype(BF16)
        cum = (jnp.dot(tri, logw_hi, preferred_element_type=F32)
               + jnp.dot(tri, logw_lo, preferred_element_type=F32))
        cum_last = cum[lc - 1:lc, :]
        for h in range(1, cpg):
            cum_last = jnp.where(row // lc == h, cum[(h + 1) * lc - 1:(h + 1) * lc, :], cum_last)
        dec_inv = jnp.exp(-cum)
        dec_tail = jnp.exp(cum_last - cum)

        big_sc[0, rows, :] = xr * jnp.exp(cum)
        big_sc[1, rows, :] = xk * vec(_V_KK)
        big_sc[2, rows, :] = iclr
        big_sc[3, rows, :] = k * dec_inv
        big_sc[4, rows, :] = v
        big_sc[5, rows, :] = jnp.exp(cum - logw)
        big_sc[6, rows, :] = k * dec_tail
        big_sc[7, rows, :] = xr * k * vec(_V_RK)
        big_sc[8, rows, :] = dec_inv
        big_sc[9, rows, :] = dec_tail
        big_sc[10, rows, :] = jnp.broadcast_to(jnp.exp(cum_last), (g_rows, width))
        return last_r, last_k, last_v, last_l

    prevs = (pr_sc[...], pk_sc[...], pv_sc[...], pl_sc[...])
    for g in range(n_grp):
        prevs = prologue(g, prevs)
    pr_sc[...], pk_sc[...], pv_sc[...], pl_sc[...] = prevs

    first = lax.broadcasted_iota(jnp.int32, (lc, pw), 1) < hd

    def head_sum(x):
        s0 = jnp.sum(jnp.where(first, x, 0.0), axis=-1, keepdims=True)
        s1 = jnp.sum(jnp.where(first, 0.0, x), axis=-1, keepdims=True)
        return jnp.where(first, s0, s1)

    n_pairs = n_heads // 2
    states = [s_sc[hp] for hp in range(n_pairs)]
    for g in range(n_grp):
        chunks = range(g * cpg, (g + 1) * cpg)
        ins, tails = [], []
        for h in chunks:
            for hp in range(n_pairs):
                rows, lanes = slice(h * lc, (h + 1) * lc), slice(hp * pw, (hp + 1) * pw)
                rt, kkraw, iclr_p, kt, v_p, dex, kh, rk, dinv, dtail, wl = [
                    big_sc[i, rows, lanes] for i in range(11)]
                kkn = kkraw * lax.rsqrt(head_sum(kkraw * kkraw) + 1e-12)
                b = kkn * iclr_p
                ins.append((rt, -kkn * dex, b * dinv, kt, v_p))
                tails.append((v_p, b * dtail, kh, wl[0:1]))
                bonus_sc[rows, lanes] = head_sum(rk) * v_p
        solved = _rwkv_solve(ins)
        for j, h in enumerate(chunks):
            sel = slice(j * n_pairs, (j + 1) * n_pairs)
            outs = _rwkv_state(solved[sel], tails[sel], states)
            states = [s_new for _, s_new in outs]
            for hp, (o, _) in enumerate(outs):
                cen = o - head_sum(o) * (1.0 / hd)
                var = head_sum(cen * cen) * (1.0 / hd)
                on_sc[h * lc:(h + 1) * lc, hp * pw:(hp + 1) * pw] = cen * lax.rsqrt(var + LNX_EPS)
    for hp in range(n_pairs):
        s_sc[hp] = states[hp]
    o_ref[...] = (on_sc[...] * vec(_V_LNW) + vec(_V_LNB) + bonus_sc[...]) * gate_sc[...]


def _rwkv(p, vfirst, vecs, mu_l, w2, a2, g2p, v2p, *, batch, width, col0, lora_col):
    rows = p.shape[0]
    tp = rows // batch
    n_sub = RWKV_SUBCHUNKS if tp % (RWKV_SUBCHUNKS * RWKV_CHUNK) == 0 else 1
    lc = n_sub * RWKV_CHUNK
    nck = tp // lc
    n_heads = width // RWKV_HEAD
    has_vres = vfirst is not None
    rowblk = lambda b, c: b * nck + c
    cb = col0 // width
    tok = lambda j: pl.BlockSpec((lc, width), lambda b, c: (rowblk(b, c), j))
    full = lambda a: pl.BlockSpec(a.shape, lambda b, c: (0,) * a.ndim)
    in_specs = [tok(cb), tok(cb + 1), tok(cb + 2),
                pl.BlockSpec((lc, LORA_BLOCK), lambda b, c: (rowblk(b, c), lora_col // LORA_BLOCK))]
    args = [p, p, p, p]
    if has_vres:
        in_specs.append(tok(0))
        args.append(vfirst)
    in_specs += [full(vecs), full(mu_l), full(w2), full(a2), full(g2p)]
    args += [vecs, mu_l, w2, a2, g2p]
    if has_vres:
        in_specs.append(full(v2p))
        args.append(v2p)
    out_shape = [jax.ShapeDtypeStruct((rows, width), F32)]
    out_specs = [tok(0)]
    if not has_vres:
        out_shape.append(jax.ShapeDtypeStruct((rows, width), F32))
        out_specs.append(tok(0))
    outs = pl.pallas_call(
        functools.partial(_rwkv_kernel, has_vres=has_vres, n_heads=n_heads),
        grid=(batch, nck),
        in_specs=in_specs,
        out_specs=out_specs,
        out_shape=out_shape,
        scratch_shapes=[pltpu.VMEM((n_heads // 2, 2 * RWKV_HEAD, 2 * RWKV_HEAD), F32),
                        pltpu.VMEM((1, width), F32), pltpu.VMEM((1, width), F32),
                        pltpu.VMEM((1, width), F32), pltpu.VMEM((1, LORA_BLOCK), F32),
                        pltpu.VMEM((11, lc, width), F32), pltpu.VMEM((lc, width), F32),
                        pltpu.VMEM((lc, width), F32), pltpu.VMEM((lc, width), F32)],
        compiler_params=_cparams("parallel", "arbitrary"),
        name="rwkv7",
    )(*args)
    return outs if not has_vres else (outs[0], None)


def _merge_kernel(x_ref, ga_ref, gb_ref, u_ref, ys_ref, yb_ref, d_ref, wglu_ref, wua_ref, wub_ref,
                  wo_ref, o_ref):
    y = ys_ref[...] + d_ref[...] * u_ref[...]
    y = jax.nn.gelu(y, approximate=True)
    ya = y * jax.nn.sigmoid(jnp.dot(y.astype(BF16), wglu_ref[...], preferred_element_type=F32))
    up_a = jnp.dot(ya.astype(BF16), wua_ref[...], preferred_element_type=F32)
    up_b = jnp.dot(yb_ref[...].astype(BF16), wub_ref[...], preferred_element_type=F32)
    merged = jax.nn.sigmoid(ga_ref[...]) * up_a + jax.nn.sigmoid(gb_ref[...]) * up_b
    o_ref[...] = x_ref[...] + jnp.dot(merged.astype(BF16), wo_ref[...], preferred_element_type=F32)


def _merge(x, p, ys, yb, d_skip, wglu, wua, wub, wo, *, u_col, layer):
    rows, d = x.shape
    w = ys.shape[1]
    tm = MERGE_ROW_TILE if rows % MERGE_ROW_TILE == 0 else min(256, _row_tile(rows))
    row = lambda width, j: pl.BlockSpec((tm, width), lambda i: (i, j))
    const = lambda a: pl.BlockSpec((None,) + a.shape[1:], lambda i: (layer, 0, 0),
                                   pipeline_mode=pl.Buffered(1))
    return pl.pallas_call(
        _merge_kernel,
        grid=(rows // tm,),
        in_specs=[row(d, 0), row(d, 0), row(d, 1), row(w, u_col // w), row(w, 0), row(w, 0),
                  pl.BlockSpec(d_skip.shape, lambda i: (0, 0)),
                  const(wglu), const(wua), const(wub), const(wo)],
        out_specs=row(d, 0),
        out_shape=jax.ShapeDtypeStruct((rows, d), F32),
        compiler_params=_cparams("parallel"),
        name="merge",
    )(x, p, p, p, ys, yb, d_skip, wglu, wua, wub, wo)


def _norm_kernel(x_ref, g_ref, o_ref):
    o_ref[...] = _rms(x_ref[...], g_ref[...])


def _final_norm(x, g, *, batch, skip, seq):
    rows, d = x.shape
    tp = rows // batch
    tm = _row_tile(seq)
    per_seq = seq // tm
    x_spec = pl.BlockSpec((pl.Element(tm), pl.Element(d)),
                          lambda i: (pl.multiple_of(
                              (i // per_seq) * tp + skip + (i % per_seq) * tm, ROW_ALIGN), 0))
    assert tp % ROW_ALIGN == 0 and skip % ROW_ALIGN == 0 and tm % ROW_ALIGN == 0
    return pl.pallas_call(
        _norm_kernel,
        grid=(batch * per_seq,),
        in_specs=[x_spec, pl.BlockSpec((1, d), lambda i: (0, 0))],
        out_specs=pl.BlockSpec((tm, d), lambda i: (i, 0)),
        out_shape=jax.ShapeDtypeStruct((batch * seq, d), F32),
        compiler_params=_cparams("parallel"),
        name="final_norm",
    )(x, g)


def _pad_rows(a, height):
    return jnp.pad(a, ((0, height - a.shape[0]), (0, 0)))


def kernel(x, meta_tokens, ffn1_norm, ffn1_w_gate, ffn1_w_up, ffn1_w_down, mix_norm, w_in_first, w_in_rest, mu_shift, mu_vres, ssm_lambda_re, ssm_lambda_im, ssm_log_dt, ssm_b_re, ssm_b_im, ssm_c_re, ssm_c_im, ssm_d, ssm_w_glu, rwkv_w0, rwkv_w2, rwkv_a0, rwkv_a2, rwkv_v0, rwkv_v2, rwkv_g2, rwkv_k_k, rwkv_k_a, rwkv_r_k, rwkv_lnx_w, rwkv_lnx_b, w_up_ssm, w_up_rwkv, w_out, ffn2_norm, ffn2_w_gate, ffn2_w_up, ffn2_w_down, final_norm):
    bsz, seq, d = x.shape
    n_meta = meta_tokens.shape[0]
    depth = ffn1_norm.shape[0]
    w = ssm_d.shape[1]
    n_groups = ssm_b_re.shape[1]
    ld, la, lg, lv = rwkv_w2.shape[1], rwkv_a2.shape[1], rwkv_g2.shape[1], rwkv_v2.shape[1]
    assert w == rwkv_w0.shape[1] and 2 * w == d and w % 128 == 0
    assert n_groups * SSM_GROUP == w and ld + la <= 128 and lg <= 256 and lv <= 128
    t_len = n_meta + seq
    tp = -(-t_len // SEQ_ALIGN) * SEQ_ALIGN
    rows = bsz * tp

    meta = jnp.broadcast_to(meta_tokens[None].astype(x.dtype), (bsz, n_meta, d))
    h_res = jnp.concatenate([meta, x, jnp.zeros((bsz, tp - t_len, d), x.dtype)], axis=1)
    h_res = h_res.reshape(rows, d)

    u_col = 2 * d
    rkv_col = u_col + w
    lora_col = rkv_col + 3 * w
    c_w = 4 * w
    c_g = c_w + ld + la
    c_ga = c_g + lg
    c_gb = c_ga + d
    p_common = c_gb + d

    ffn1_w = [a.astype(BF16) for a in (ffn1_w_gate, ffn1_w_up, ffn1_w_down)]
    ffn2_w = [a.astype(BF16) for a in (ffn2_w_gate, ffn2_w_up, ffn2_w_down)]
    merge_w = [a.astype(BF16) for a in (ssm_w_glu, w_up_ssm, w_up_rwkv, w_out)]

    w_all = jnp.concatenate([jnp.pad(w_in_first, ((0, 0), (0, lv)))[None], w_in_rest], axis=0)
    pad3 = lambda a, width: jnp.pad(a, ((0, 0), (0, 0), (0, width - a.shape[2])))
    w_in = jnp.concatenate(
        [w_all[:, :, c_ga:c_gb], w_all[:, :, c_gb:p_common], w_all[:, :, :c_w],
         pad3(w_all[:, :, c_w:c_g], 128), pad3(w_all[:, :, c_g:c_ga], 256),
         pad3(w_all[:, :, p_common:], 128)], axis=2).astype(BF16)

    v_first = None
    for i in range(depth):
        h_res = _ffn(h_res, ffn1_norm[i][None], *ffn1_w, i)

        p, u = _proj(h_res, mix_norm[i][None], w_in, i, u_col, w)

        ys = _s5_scan(u, ssm_log_dt[i], ssm_lambda_re[i], ssm_lambda_im[i],
                      ssm_c_re[i], ssm_c_im[i], ssm_b_re[i], ssm_b_im[i], batch=bsz)

        mu = mu_shift[i]
        zero_w = jnp.zeros((w,), F32)
        vecs = jnp.stack([rwkv_w0[i], rwkv_a0[i], rwkv_v0[i - 1] if i else zero_w, rwkv_k_k[i],
                          rwkv_k_a[i], rwkv_r_k[i].reshape(w), rwkv_lnx_w[i], rwkv_lnx_b[i],
                          mu[0:w], mu[w:2 * w], mu[2 * w:3 * w]])
        vecs = _pad_rows(vecs, _VEC_ROWS)
        mu_v = mu_vres[i - 1] if i else jnp.zeros((lv,), F32)
        mu_l = jnp.concatenate([jnp.pad(mu[3 * w:3 * w + ld + la], (0, 128 - ld - la)),
                                jnp.pad(mu[3 * w + ld + la:], (0, 256 - lg)),
                                jnp.pad(mu_v, (0, 128 - lv))])[None]
        w2p = _pad_rows(rwkv_w2[i], 128).astype(BF16)
        a2p = jnp.pad(rwkv_a2[i], ((ld, 128 - ld - la), (0, 0))).astype(BF16)
        g2p = _pad_rows(rwkv_g2[i], 256).astype(BF16)
        v2p = _pad_rows(rwkv_v2[i - 1], 128).astype(BF16) if i else None
        yb, v_new = _rwkv(p, v_first, vecs, mu_l, w2p, a2p, g2p, v2p,
                          batch=bsz, width=w, col0=rkv_col, lora_col=lora_col)
        if i == 0:
            v_first = v_new

        h_res = _merge(h_res, p, ys, yb, ssm_d[i][None], *merge_w, u_col=u_col, layer=i)

        h_res = _ffn(h_res, ffn2_norm[i][None], *ffn2_w, i)

    out = _final_norm(h_res, final_norm[None], batch=bsz, skip=n_meta, seq=seq)
    return out.reshape(bsz, seq, d)
```

```python
import functools
import math

import jax
import jax.numpy as jnp
from jax import lax
from jax.experimental import pallas as pl
from jax.experimental.pallas import tpu as pltpu

F32 = jnp.float32
BF16 = jnp.bfloat16
HIGHEST = lax.Precision.HIGHEST

NORM_EPS = 1e-6
RWKV_HEAD = 64
LNX_EPS = RWKV_HEAD * 1e-5
SSM_GROUP = 16
SCAN_GROUP = 8
S5_CHUNK = 16
RWKV_CHUNK = 64
RWKV_SUBCHUNKS = 3
RWKV_GROUPS = 1
FFN_ROW_TILE = 1056
MERGE_ROW_TILE = 384
PROJ_ROW_TILE = 1536
LORA_BLOCK = 512
ROW_ALIGN = 8
SEQ_ALIGN = 128
VMEM_LIMIT_BYTES = 56 * 1024 * 1024


def _cparams(*sem):
    return pltpu.CompilerParams(dimension_semantics=sem, vmem_limit_bytes=VMEM_LIMIT_BYTES)


def _rms(x, g):
    inv = lax.rsqrt(jnp.mean(x * x, axis=-1, keepdims=True) + NORM_EPS)
    return (x * inv) * g


def _row_tile(rows):
    for tm in (512, 256, 128, 64, 32, 16, 8):
        if rows % tm == 0:
            return tm
    raise ValueError(rows)


def _col_tile(cols):
    for tn in (512, 256, 128):
        if cols % tn == 0:
            return tn
    raise ValueError(cols)


def _ffn_kernel(x_ref, g_ref, wg_ref, wu_ref, wd_ref, o_ref, h_sc):
    f = pl.program_id(1)

    @pl.when(f == 0)
    def _():
        x = x_ref[...]
        h_sc[...] = _rms(x, g_ref[...]).astype(BF16)
        o_ref[...] = x

    h = h_sc[...]
    a = jnp.dot(h, wg_ref[...], preferred_element_type=F32)
    b = jnp.dot(h, wu_ref[...], preferred_element_type=F32)
    z = (0.5 * (a * jax.nn.sigmoid(a)) * b).astype(BF16)
    o_ref[...] += jnp.dot(z, wd_ref[...], preferred_element_type=F32)


def _ffn(x, g, wg, wu, wd, layer):
    rows, d = x.shape
    dff = wg.shape[2]
    tm = FFN_ROW_TILE if rows % FFN_ROW_TILE == 0 else _row_tile(rows)
    tf = _col_tile(dff)
    return pl.pallas_call(
        _ffn_kernel,
        grid=(rows // tm, dff // tf),
        in_specs=[
            pl.BlockSpec((tm, d), lambda i, f: (i, 0)),
            pl.BlockSpec((1, d), lambda i, f: (0, 0)),
            pl.BlockSpec((None, d, tf), lambda i, f: (layer, 0, f)),
            pl.BlockSpec((None, d, tf), lambda i, f: (layer, 0, f)),
            pl.BlockSpec((None, tf, d), lambda i, f: (layer, f, 0)),
        ],
        out_specs=pl.BlockSpec((tm, d), lambda i, f: (i, 0)),
        out_shape=jax.ShapeDtypeStruct((rows, d), F32),
        scratch_shapes=[pltpu.VMEM((tm, d), BF16)],
        compiler_params=_cparams("parallel", "arbitrary"),
        name="ffn",
    )(x, g, wg, wu, wd)


def _proj_kernel(x_ref, g_ref, w_ref, o_ref, h_sc):
    @pl.when(pl.program_id(1) == 0)
    def _():
        h_sc[...] = _rms(x_ref[...], g_ref[...]).astype(BF16)

    o_ref[...] = jnp.dot(h_sc[...], w_ref[...], preferred_element_type=F32)


def _proj_split_kernel(x_ref, g_ref, w_ref, o_ref, u_ref, h_sc, *, j0, nu):
    _proj_kernel(x_ref, g_ref, w_ref, o_ref, h_sc)
    j = pl.program_id(1)

    @pl.when(jnp.logical_and(j >= j0, j < j0 + nu))
    def _():
        u_ref[...] = o_ref[...]


def _proj(x, g, w, layer, u_col, u_width):
    rows, d = x.shape
    n = w.shape[2]
    tm = PROJ_ROW_TILE if rows % PROJ_ROW_TILE == 0 else _row_tile(rows)
    tn = _col_tile(n)
    split = u_col % tn == 0 and u_width % tn == 0
    in_specs = [
        pl.BlockSpec((tm, d), lambda i, j: (i, 0)),
        pl.BlockSpec((1, d), lambda i, j: (0, 0)),
        pl.BlockSpec((None, d, tn), lambda i, j: (layer, 0, j)),
    ]
    p_spec = pl.BlockSpec((tm, tn), lambda i, j: (i, j))
    p_shape = jax.ShapeDtypeStruct((rows, n), F32)
    common = dict(grid=(rows // tm, n // tn), in_specs=in_specs,
                  scratch_shapes=[pltpu.VMEM((tm, d), BF16)],
                  compiler_params=_cparams("parallel", "arbitrary"), name="in_proj")
    if not split:
        p = pl.pallas_call(_proj_kernel, out_specs=p_spec, out_shape=p_shape, **common)(x, g, w)
        return p, p[:, u_col:u_col + u_width]
    j0, nu = u_col // tn, u_width // tn
    u_spec = pl.BlockSpec((tm, tn), lambda i, j: (i, jnp.clip(j - j0, 0, nu - 1)))
    return pl.pallas_call(
        functools.partial(_proj_split_kernel, j0=j0, nu=nu),
        out_specs=[p_spec, u_spec],
        out_shape=[p_shape, jax.ShapeDtypeStruct((rows, u_width), F32)],
        **common)(x, g, w)


def _cmul(ar, ai, br, bi):
    return ar * br - ai * bi, ar * bi + ai * br


def _cpow_int(a_re, a_im, expo, n_bits):
    shape = jnp.broadcast_shapes(a_re.shape, expo.shape)
    p_re, p_im = jnp.ones(shape, F32), jnp.zeros(shape, F32)
    for bit in range(n_bits):
        take = jnp.bitwise_and(lax.shift_right_logical(expo, bit), 1) == 1
        q_re, q_im = _cmul(p_re, p_im, a_re, a_im)
        p_re, p_im = jnp.where(take, q_re, p_re), jnp.where(take, q_im, p_im)
        if bit + 1 < n_bits:
            a_re, a_im = _cmul(a_re, a_im, a_re, a_im)
    return p_re, p_im


def _s5_param_kernel(ldt_ref, lrc_ref, lic_ref, lrr_ref, lir_ref, ctre_ref, ctim_ref,
                     btre_ref, btim_ref, bttre_ref, bttim_ref, m0_ref, e_ref, p_ref, ap_ref,
                     *, n_chunks):
    lc = S5_CHUNK
    cw = SSM_GROUP
    width = lc * cw
    n_bits = (lc - 1).bit_length()
    dt = jnp.exp(ldt_ref[...])

    def abar(lr, li):
        mag = jnp.exp(lr * dt)
        return mag * jnp.cos(li * dt), mag * jnp.sin(li * dt)

    lr_r, li_r = lrr_ref[...], lir_ref[...]
    ac_re, ac_im = abar(lrc_ref[...], lic_ref[...])
    ar_re, ar_im = abar(lr_r, li_r)

    den = lr_r * lr_r + li_r * li_r
    nr, ni = ar_re - 1.0, ar_im
    q_re = (nr * lr_r + ni * li_r) / den
    q_im = (ni * lr_r - nr * li_r) / den

    lag = lax.broadcasted_iota(jnp.int32, (1, width), 1) // cw
    ct_re, ct_im = ctre_ref[...], ctim_ref[...]
    p_re, p_im = _cpow_int(ac_re, ac_im, lag, n_bits)
    d0_re, d0_im = _cmul(ct_re, ct_im, p_re, p_im)
    bt_re, bt_im = btre_ref[...], btim_ref[...]
    bbt_re, bbt_im = _cmul(q_re, q_im, bt_re, bt_im)
    m0_ref[...] = (jnp.dot(bbt_re, d0_re, preferred_element_type=F32, precision=HIGHEST)
                   - jnp.dot(bbt_im, d0_im, preferred_element_type=F32, precision=HIGHEST))

    rowj = lax.broadcasted_iota(jnp.int32, (width, 1), 0) // cw
    e_pre, e_pim = _cpow_int(ar_re, ar_im, (lc - 1) - rowj, n_bits)
    btt_re, btt_im = bttre_ref[...], bttim_ref[...]
    bb2_re, bb2_im = _cmul(q_re, q_im, btt_re, btt_im)
    e_re, e_im = _cmul(bb2_re, bb2_im, e_pre, e_pim)
    e_ref[...] = jnp.concatenate([e_re, e_im], axis=1)

    p1_re, p1_im = _cmul(p_re, p_im, ac_re, ac_im)
    d1_re, d1_im = _cmul(ct_re, ct_im, p1_re, p1_im)
    p_ref[...] = jnp.concatenate([d1_re, -d1_im], axis=0)

    ap_ref[...] = jnp.zeros_like(ap_ref)
    a_re, a_im = ar_re, ar_im
    for _ in range(n_bits):
        a_re, a_im = _cmul(a_re, a_im, a_re, a_im)
    assert 1 << n_bits == lc
    pre, pim = a_re, a_im
    for i, _ in enumerate(_scan_shifts(SCAN_GROUP)):
        ap_ref[2 * i:2 * i + 1, :] = jnp.concatenate([pre, pre], axis=1)
        ap_ref[2 * i + 1:2 * i + 2, :] = jnp.concatenate([-pim, pim], axis=1)
        pre, pim = _cmul(pre, pim, pre, pim)
    pre, pim = a_re, a_im
    for j in range(SCAN_GROUP):
        ap_ref[SCAN_GROUP + j:SCAN_GROUP + j + 1, :] = jnp.concatenate([pre, pre], axis=1)
        ap_ref[2 * SCAN_GROUP + j:2 * SCAN_GROUP + j + 1, :] = jnp.concatenate([-pim, pim], axis=1)
        pre, pim = _cmul(pre, pim, a_re, a_im)


def _scan_shifts(n_chunks):
    shifts, s = [], 1
    while s < n_chunks:
        shifts.append(s)
        s *= 2
    return shifts


def _s5_main_kernel(*refs, n_chunks, n_state):
    lc = S5_CHUNK
    cw = SSM_GROUP
    x_refs = refs[:lc]
    m0_ref, e_ref, p_ref, ap_ref, y_ref, m_sc, e_sc, p_sc, ap_sc = refs[lc:]
    lb = x_refs[0].shape[1]
    gpb = lb // cw
    width = lc * cw
    sw = 2 * n_state

    @pl.when(pl.program_id(1) == 0)
    def _():
        def iota(shape, dim):
            return lax.broadcasted_iota(jnp.int32, shape, dim)

        def spread(shape, dim_small):
            small, big = iota(shape, dim_small), iota(shape, 1 - dim_small)
            return jnp.logical_and(small // cw == big // lb, small % cw == big % cw)

        lane_expand = spread((width, lc * lb), 0).astype(BF16)
        row_expand = spread((lc * lb, width), 1).astype(BF16)

        m0w = jnp.dot(m0_ref[...].astype(BF16), lane_expand, preferred_element_type=F32)
        own = iota(m0w.shape, 0) // cw == (iota(m0w.shape, 1) % lb) // cw
        m0 = jnp.where(own, m0w, 0.0).astype(BF16)
        for t in range(lc):
            blk = m0 if t == 0 else jnp.concatenate(
                [jnp.zeros((lb, t * lb), m0.dtype), m0[:, :(lc - t) * lb]], axis=1)
            m_sc[t * lb:(t + 1) * lb, :] = blk

        e_stack = jnp.concatenate([e_ref[g] for g in range(gpb)], axis=1).astype(BF16)
        e_full = jnp.dot(row_expand, e_stack, preferred_element_type=F32)
        own = (iota(e_full.shape, 0) % lb) // cw == iota(e_full.shape, 1) // sw
        e_sc[...] = jnp.where(own, e_full, 0.0).astype(BF16)

        p_stack = p_ref[...].reshape(gpb * sw, width).astype(BF16)
        p_full = jnp.dot(p_stack, lane_expand, preferred_element_type=F32)
        own = iota(p_full.shape, 0) // sw == (iota(p_full.shape, 1) % lb) // cw
        p_sc[...] = jnp.where(own, p_full, 0.0).astype(BF16)

        ap_sc[...] = jnp.concatenate([ap_ref[g] for g in range(gpb)], axis=1)

    x = jnp.concatenate([r[...].astype(BF16) for r in x_refs], axis=1)
    y = jnp.concatenate(
        [jnp.dot(x[:, :(q + 2) * lb], m_sc[:(q + 2) * lb, q * lb:(q + 2) * lb],
                 preferred_element_type=F32) for q in range(0, lc, 2)], axis=1)
    s = jnp.dot(x, e_sc[...], preferred_element_type=F32)

    def swap_re_im(a):
        return jnp.concatenate([pltpu.roll(a[:, j * sw:(j + 1) * sw], n_state, axis=1)
                                for j in range(a.shape[1] // sw)], axis=1)

    sg = SCAN_GROUP
    crow = lax.broadcasted_iota(jnp.int32, (n_chunks, 1), 0)
    in_group = crow % sg
    for i, shift in enumerate(_scan_shifts(sg)):
        prev = jnp.where(in_group >= shift, pltpu.roll(s, shift, axis=0), 0.0)
        s = s + ap_sc[2 * i:2 * i + 1, :] * prev + ap_sc[2 * i + 1:2 * i + 2, :] * swap_re_im(prev)
    pow_a, pow_b = ap_sc[sg:2 * sg, :], ap_sc[2 * sg:3 * sg, :]
    s_sw = swap_re_im(s)
    groups = [s[r * sg:(r + 1) * sg] for r in range(n_chunks // sg)]
    groups_sw = [s_sw[r * sg:(r + 1) * sg] for r in range(n_chunks // sg)]
    for r in range(1, len(groups)):
        c, c_sw = groups[r - 1][sg - 1:sg, :], groups_sw[r - 1][sg - 1:sg, :]
        groups[r] = groups[r] + pow_a * c + pow_b * c_sw
        groups_sw[r] = groups_sw[r] + pow_a * c_sw - pow_b * c
    s = jnp.concatenate(groups, axis=0)
    s_start = jnp.where(crow >= 1, pltpu.roll(s, 1, axis=0), 0.0)
    y = y + jnp.dot(s_start.astype(BF16), p_sc[...], preferred_element_type=F32)
    for t in range(lc):
        y_ref[pl.ds(t, n_chunks, stride=lc), :] = y[:, t * lb:(t + 1) * lb]


def _s5_scan(u, ldt, lr, li, c_re, c_im, b_re, b_im, *, batch):
    rows, w = u.shape
    g, n = lr.shape
    cw = SSM_GROUP
    lc = S5_CHUNK
    width = lc * cw
    lb = 128
    gpb = lb // cw
    n_oct = w // lb
    n_chunks = rows // batch // lc
    assert n_chunks % SCAN_GROUP == 0 and 2 * len(_scan_shifts(SCAN_GROUP)) <= SCAN_GROUP
    ap_rows = 3 * SCAN_GROUP
    ct_re = jnp.tile(jnp.swapaxes(c_re, 1, 2), (1, 1, lc))
    ct_im = jnp.tile(jnp.swapaxes(c_im, 1, 2), (1, 1, lc))
    bt_re = jnp.swapaxes(b_re, 1, 2)
    bt_im = jnp.swapaxes(b_im, 1, 2)
    btt_re = jnp.tile(bt_re, (1, lc, 1))
    btt_im = jnp.tile(bt_im, (1, lc, 1))
    g3 = lambda a, b: pl.BlockSpec((None, a, b), lambda i: (i, 0, 0))
    m0, e_g, p_g, ap = pl.pallas_call(
        functools.partial(_s5_param_kernel, n_chunks=n_chunks),
        grid=(g,),
        in_specs=[g3(1, 1), g3(n, 1), g3(n, 1), g3(1, n), g3(1, n),
                  g3(n, width), g3(n, width), g3(cw, n), g3(cw, n), g3(width, n), g3(width, n)],
        out_specs=[g3(cw, width), g3(width, 2 * n), g3(2 * n, width), g3(ap_rows, 2 * n)],
        out_shape=[jax.ShapeDtypeStruct((g, cw, width), F32),
                   jax.ShapeDtypeStruct((g, width, 2 * n), F32),
                   jax.ShapeDtypeStruct((g, 2 * n, width), F32),
                   jax.ShapeDtypeStruct((g, ap_rows, 2 * n), F32)],
        compiler_params=_cparams("parallel"),
        name="s5_params",
    )(ldt.reshape(g, 1, 1), lr.reshape(g, n, 1), li.reshape(g, n, 1), lr.reshape(g, 1, n),
      li.reshape(g, 1, n), ct_re, ct_im, bt_re, bt_im, btt_re, btt_im)

    m0_o = m0.reshape(n_oct, lb, width)
    e_o = e_g.reshape(n_oct, gpb, width, 2 * n)
    p_o = p_g.reshape(n_oct, gpb, 2 * n, width)
    ap_o = ap.reshape(n_oct, gpb, ap_rows, 2 * n)

    u2 = u.reshape(rows // lc, lc * w)
    xspec = lambda t: pl.BlockSpec((n_chunks, lb), lambda o, b: (b, t * n_oct + o))
    per_oct = lambda a: pl.BlockSpec((None,) + a.shape[1:], lambda o, b: (o,) + (0,) * (a.ndim - 1))
    ys = pl.pallas_call(
        functools.partial(_s5_main_kernel, n_chunks=n_chunks, n_state=n),
        grid=(n_oct, batch),
        in_specs=[xspec(t) for t in range(lc)] + [per_oct(m0_o), per_oct(e_o), per_oct(p_o),
                                                   per_oct(ap_o)],
        out_specs=pl.BlockSpec((n_chunks * lc, lb), lambda o, b: (b, o)),
        out_shape=jax.ShapeDtypeStruct((rows, w), F32),
        scratch_shapes=[pltpu.VMEM((lc * lb, lc * lb), BF16),
                        pltpu.VMEM((lc * lb, gpb * 2 * n), BF16),
                        pltpu.VMEM((gpb * 2 * n, lc * lb), BF16),
                        pltpu.VMEM((ap_rows, gpb * 2 * n), F32)],
        compiler_params=_cparams("parallel", "arbitrary"),
        name="s5_scan",
    )(*([u2] * lc), m0_o, e_o, p_o, ap_o)
    return ys


(_V_W0, _V_A0, _V_V0, _V_KK, _V_KA, _V_RK, _V_LNW, _V_LNB, _V_MUR, _V_MUK, _V_MUV) = range(11)
_VEC_ROWS = 16


def _dot_nt(a, b, **kw):
    return lax.dot_general(a, b, (((1,), (1,)), ((), ())), preferred_element_type=F32, **kw)


def _dot_tn(a, b, **kw):
    return lax.dot_general(a, b, (((0,), (0,)), ((), ())), preferred_element_type=F32, **kw)


def _mm(a, b):
    return jnp.dot(a.astype(BF16), b.astype(BF16), preferred_element_type=F32)


def _rwkv_solve(pairs):
    lc, pw = pairs[0][0].shape
    hd = RWKV_HEAD
    assert lc == hd and pw == 2 * hd
    lane = lax.broadcasted_iota(jnp.int32, (lc, pw), 1)
    row = lax.broadcasted_iota(jnp.int32, (lc, pw), 0)
    first = lane < hd
    col = jnp.where(first, lane, lane - hd)
    strict2 = row > col
    incl2 = row >= col
    ak_mask = jnp.logical_and(strict2, jnp.logical_not(first))
    zeros = jnp.zeros((lc, pw), F32)
    heads = [(p, e) for p in range(len(pairs)) for e in range(2)]

    gms = []
    for rt, at, bt, kt, v in pairs:
        ar = jnp.concatenate([jnp.where(first, at, 0.0), jnp.where(first, rt, 0.0),
                              jnp.where(first, 0.0, at), jnp.where(first, 0.0, rt)], axis=0)
        gms.append(_dot_nt(ar.astype(BF16), jnp.concatenate([bt, kt], axis=0).astype(BF16)))
    ga = [gms[p][2 * e * lc:(2 * e + 1) * lc] for p, e in heads]
    gr = [gms[p][(2 * e + 1) * lc:(2 * e + 2) * lc] for p, e in heads]
    vr = [pltpu.roll(pr[4], hd, axis=1) for pr in pairs]
    vz = [jnp.concatenate([zeros, r], axis=0).astype(BF16) for r in vr]

    nk = [jnp.where(strict2[:, :lc], g[:, :lc], 0.0).astype(BF16) for g in ga]
    y = []
    for i, (p, e) in enumerate(heads):
        akv = _mm(jnp.where(ak_mask, ga[i], 0.0), vz[p])
        y.append(jnp.where(first, pairs[p][1], akv) if e == 0 else jnp.where(first, akv, pairs[p][1]))
    steps = max(1, (lc - 1).bit_length())
    for i in range(steps):
        y = [yy + _mm(nn, yy) for nn, yy in zip(nk, y)]
        if i + 1 < steps:
            nk = [_mm(nn, nn).astype(BF16) for nn in nk]
    rz = []
    for i, (p, e) in enumerate(heads):
        zv = jnp.where(first, 0.0, vr[p]) if e == 0 else jnp.where(first, vr[p], 0.0)
        rz.append(_mm(jnp.where(incl2, gr[i], 0.0), jnp.concatenate([y[i], zv], axis=0)))

    outs = []
    for p, (rt, at, bt, kt, v) in enumerate(pairs):
        y0, y1, rz0, rz1 = y[2 * p], y[2 * p + 1], rz[2 * p], rz[2 * p + 1]
        outs.append((jnp.where(first, y0, y1),
                     rt + jnp.where(first, rz0, rz1),
                     pltpu.roll(jnp.where(first, y1, y0), hd, axis=1),
                     pltpu.roll(jnp.where(first, rz1, rz0), hd, axis=1)))
    return outs


def _rwkv_state(solved, tails, states):
    lc, pw = solved[0][0].shape
    hd = RWKV_HEAD
    uos = [_mm(jnp.concatenate([x1, z1], axis=0), s0)
           for (x1, z1, x2, z2), s0 in zip(solved, states)]
    si = lax.broadcasted_iota(jnp.int32, (pw, pw), 0)
    sj = lax.broadcasted_iota(jnp.int32, (pw, pw), 1)
    same_head = (si < hd) == (sj < hd)
    outs = []
    for uo, (x1, z1, x2, z2), (v, bh, kh, wl), s0 in zip(uos, solved, tails, states):
        u = uo[:lc] + x2
        o = uo[lc:] + z2
        upd = _dot_tn(jnp.concatenate([bh, kh], axis=0).astype(BF16),
                      jnp.concatenate([u, v], axis=0).astype(BF16))
        decay = jnp.broadcast_to(wl, (pw, pw)).T
        outs.append((o, s0 * decay + jnp.where(same_head, upd, 0.0)))
    return outs


def _rwkv_kernel(*refs, has_vres, n_heads):
    if has_vres:
        (r_ref, k_ref, v_ref, l_ref, vf_ref, vec_ref, mul_ref, w2_ref, a2_ref, g2_ref, v2_ref,
         o_ref, s_sc, pr_sc, pk_sc, pv_sc, pl_sc, big_sc, on_sc, bonus_sc, gate_sc) = refs
        vout_ref = None
    else:
        (r_ref, k_ref, v_ref, l_ref, vec_ref, mul_ref, w2_ref, a2_ref, g2_ref,
         o_ref, vout_ref, s_sc, pr_sc, pk_sc, pv_sc, pl_sc, big_sc, on_sc, bonus_sc, gate_sc) = refs
    nr, width = r_ref.shape
    lc = RWKV_CHUNK
    n_sub = nr // lc
    hd = RWKV_HEAD
    pw = 2 * hd

    @pl.when(pl.program_id(1) == 0)
    def _():
        s_sc[...] = jnp.zeros_like(s_sc)
        pr_sc[...] = jnp.zeros_like(pr_sc)
        pk_sc[...] = jnp.zeros_like(pk_sc)
        pv_sc[...] = jnp.zeros_like(pv_sc)
        pl_sc[...] = jnp.zeros_like(pl_sc)

    def vec(i):
        return vec_ref[i:i + 1, :]

    n_grp = RWKV_GROUPS if n_sub % RWKV_GROUPS == 0 else 1
    cpg = n_sub // n_grp
    g_rows = cpg * lc

    def prologue(g, prevs):
        rows = slice(g * g_rows, (g + 1) * g_rows)
        row = lax.broadcasted_iota(jnp.int32, (g_rows, 1), 0)

        def shift_mix(x_ref, prev, mu):
            x = x_ref[rows, :]
            before = jnp.where(row == 0, prev, pltpu.roll(x, 1, axis=0))
            return x + mu * (before - x), x[g_rows - 1:g_rows, :]

        xr, last_r = shift_mix(r_ref, prevs[0], vec(_V_MUR))
        xk, last_k = shift_mix(k_ref, prevs[1], vec(_V_MUK))
        xv, last_v = shift_mix(v_ref, prevs[2], vec(_V_MUV))
        xl, last_l = shift_mix(l_ref, prevs[3], mul_ref[...])
        xwa, xg, xvr = xl[:, 0:128], xl[:, 128:384], xl[:, 384:512]

        logw = (-math.exp(-0.5)) * jax.nn.sigmoid(vec(_V_W0) + _mm(jnp.tanh(xwa), w2_ref[...]))
        iclr = jax.nn.sigmoid(vec(_V_A0) + _mm(xwa, a2_ref[...]))
        gate_sc[rows, :] = _mm(jax.nn.sigmoid(xg), g2_ref[...])
        if has_vres:
            v = xv + (vf_ref[rows, :] - xv) * jax.nn.sigmoid(vec(_V_V0) + _mm(xvr, v2_ref[...]))
        else:
            v = xv
            vout_ref[rows, :] = xv
        k = xk * (1.0 + (iclr - 1.0) * vec(_V_KA))

        ti = lax.broadcasted_iota(jnp.int32, (g_rows, g_rows), 0)
        tj = lax.broadcasted_iota(jnp.int32, (g_rows, g_rows), 1)
        tri = jnp.logical_and(ti >= tj, ti // lc == tj // lc)
        logw_hi = logw.astype(BF16)
        logw_lo = (logw - logw_hi.astype(F32)).astype(BF16)
        tri = tri.astype(BF16)
        cum = (jnp.dot(tri, logw_hi, preferred_element_type=F32)
               + jnp.dot(tri, logw_lo, preferred_element_type=F32))
        cum_last = cum[lc - 1:lc, :]
        for h in range(1, cpg):
            cum_last = jnp.where(row // lc == h, cum[(h + 1) * lc - 1:(h + 1) * lc, :], cum_last)
        dec_inv = jnp.exp(-cum)
        dec_tail = jnp.exp(cum_last - cum)

        big_sc[0, rows, :] = xr * jnp.exp(cum)
        big_sc[1, rows, :] = xk * vec(_V_KK)
        big_sc[2, rows, :] = iclr
        big_sc[3, rows, :] = k * dec_inv
        big_sc[4, rows, :] = v
        big_sc[5, rows, :] = jnp.exp(cum - logw)
        big_sc[6, rows, :] = k * dec_tail
        big_sc[7, rows, :] = xr * k * vec(_V_RK)
        big_sc[8, rows, :] = dec_inv
        big_sc[9, rows, :] = dec_tail
        big_sc[10, rows, :] = jnp.broadcast_to(jnp.exp(cum_last), (g_rows, width))
        return last_r, last_k, last_v, last_l

    prevs = (pr_sc[...], pk_sc[...], pv_sc[...], pl_sc[...])
    for g in range(n_grp):
        prevs = prologue(g, prevs)
    pr_sc[...], pk_sc[...], pv_sc[...], pl_sc[...] = prevs

    first = lax.broadcasted_iota(jnp.int32, (lc, pw), 1) < hd

    def head_sum(x):
        s0 = jnp.sum(jnp.where(first, x, 0.0), axis=-1, keepdims=True)
        s1 = jnp.sum(jnp.where(first, 0.0, x), axis=-1, keepdims=True)
        return jnp.where(first, s0, s1)

    n_pairs = n_heads // 2
    states = [s_sc[hp] for hp in range(n_pairs)]
    for g in range(n_grp):
        chunks = range(g * cpg, (g + 1) * cpg)
        ins, tails = [], []
        for h in chunks:
            for hp in range(n_pairs):
                rows, lanes = slice(h * lc, (h + 1) * lc), slice(hp * pw, (hp + 1) * pw)
                rt, kkraw, iclr_p, kt, v_p, dex, kh, rk, dinv, dtail, wl = [
                    big_sc[i, rows, lanes] for i in range(11)]
                kkn = kkraw * lax.rsqrt(head_sum(kkraw * kkraw) + 1e-12)
                b = kkn * iclr_p
                ins.append((rt, -kkn * dex, b * dinv, kt, v_p))
                tails.append((v_p, b * dtail, kh, wl[0:1]))
                bonus_sc[rows, lanes] = head_sum(rk) * v_p
        solved = _rwkv_solve(ins)
        for j, h in enumerate(chunks):
            sel = slice(j * n_pairs, (j + 1) * n_pairs)
            outs = _rwkv_state(solved[sel], tails[sel], states)
            states = [s_new for _, s_new in outs]
            for hp, (o, _) in enumerate(outs):
                cen = o - head_sum(o) * (1.0 / hd)
                var = head_sum(cen * cen) * (1.0 / hd)
                on_sc[h * lc:(h + 1) * lc, hp * pw:(hp + 1) * pw] = cen * lax.rsqrt(var + LNX_EPS)
    for hp in range(n_pairs):
        s_sc[hp] = states[hp]
    o_ref[...] = (on_sc[...] * vec(_V_LNW) + vec(_V_LNB) + bonus_sc[...]) * gate_sc[...]


def _rwkv(p, vfirst, vecs, mu_l, w2, a2, g2p, v2p, *, batch, width, col0, lora_col):
    rows = p.shape[0]
    tp = rows // batch
    n_sub = RWKV_SUBCHUNKS if tp % (RWKV_SUBCHUNKS * RWKV_CHUNK) == 0 else 1
    lc = n_sub * RWKV_CHUNK
    nck = tp // lc
    n_heads = width // RWKV_HEAD
    has_vres = vfirst is not None
    rowblk = lambda b, c: b * nck + c
    cb = col0 // width
    tok = lambda j: pl.BlockSpec((lc, width), lambda b, c: (rowblk(b, c), j))
    full = lambda a: pl.BlockSpec(a.shape, lambda b, c: (0,) * a.ndim)
    in_specs = [tok(cb), tok(cb + 1), tok(cb + 2),
                pl.BlockSpec((lc, LORA_BLOCK), lambda b, c: (rowblk(b, c), lora_col // LORA_BLOCK))]
    args = [p, p, p, p]
    if has_vres:
        in_specs.append(tok(0))
        args.append(vfirst)
    in_specs += [full(vecs), full(mu_l), full(w2), full(a2), full(g2p)]
    args += [vecs, mu_l, w2, a2, g2p]
    if has_vres:
        in_specs.append(full(v2p))
        args.append(v2p)
    out_shape = [jax.ShapeDtypeStruct((rows, width), F32)]
    out_specs = [tok(0)]
    if not has_vres:
        out_shape.append(jax.ShapeDtypeStruct((rows, width), F32))
        out_specs.append(tok(0))
    outs = pl.pallas_call(
        functools.partial(_rwkv_kernel, has_vres=has_vres, n_heads=n_heads),
        grid=(batch, nck),
        in_specs=in_specs,
        out_specs=out_specs,
        out_shape=out_shape,
        scratch_shapes=[pltpu.VMEM((n_heads // 2, 2 * RWKV_HEAD, 2 * RWKV_HEAD), F32),
                        pltpu.VMEM((1, width), F32), pltpu.VMEM((1, width), F32),
                        pltpu.VMEM((1, width), F32), pltpu.VMEM((1, LORA_BLOCK), F32),
                        pltpu.VMEM((11, lc, width), F32), pltpu.VMEM((lc, width), F32),
                        pltpu.VMEM((lc, width), F32), pltpu.VMEM((lc, width), F32)],
        compiler_params=_cparams("parallel", "arbitrary"),
        name="rwkv7",
    )(*args)
    return outs if not has_vres else (outs[0], None)


def _merge_kernel(x_ref, ga_ref, gb_ref, u_ref, ys_ref, yb_ref, d_ref, wglu_ref, wua_ref, wub_ref,
                  wo_ref, o_ref):
    y = ys_ref[...] + d_ref[...] * u_ref[...]
    y = jax.nn.gelu(y, approximate=True)
    ya = y * jax.nn.sigmoid(jnp.dot(y.astype(BF16), wglu_ref[...], preferred_element_type=F32))
    up_a = jnp.dot(ya.astype(BF16), wua_ref[...], preferred_element_type=F32)
    up_b = jnp.dot(yb_ref[...].astype(BF16), wub_ref[...], preferred_element_type=F32)
    merged = jax.nn.sigmoid(ga_ref[...]) * up_a + jax.nn.sigmoid(gb_ref[...]) * up_b
    o_ref[...] = x_ref[...] + jnp.dot(merged.astype(BF16), wo_ref[...], preferred_element_type=F32)


def _merge(x, p, ys, yb, d_skip, wglu, wua, wub, wo, *, u_col, layer):
    rows, d = x.shape
    w = ys.shape[1]
    tm = MERGE_ROW_TILE if rows % MERGE_ROW_TILE == 0 else min(256, _row_tile(rows))
    row = lambda width, j: pl.BlockSpec((tm, width), lambda i: (i, j))
    const = lambda a: pl.BlockSpec((None,) + a.shape[1:], lambda i: (layer, 0, 0),
                                   pipeline_mode=pl.Buffered(1))
    return pl.pallas_call(
        _merge_kernel,
        grid=(rows // tm,),
        in_specs=[row(d, 0), row(d, 0), row(d, 1), row(w, u_col // w), row(w, 0), row(w, 0),
                  pl.BlockSpec(d_skip.shape, lambda i: (0, 0)),
                  const(wglu), const(wua), const(wub), const(wo)],
        out_specs=row(d, 0),
        out_shape=jax.ShapeDtypeStruct((rows, d), F32),
        compiler_params=_cparams("parallel"),
        name="merge",
    )(x, p, p, p, ys, yb, d_skip, wglu, wua, wub, wo)


def _norm_kernel(x_ref, g_ref, o_ref):
    o_ref[...] = _rms(x_ref[...], g_ref[...])


def _final_norm(x, g, *, batch, skip, seq):
    rows, d = x.shape
    tp = rows // batch
    tm = _row_tile(seq)
    per_seq = seq // tm
    x_spec = pl.BlockSpec((pl.Element(tm), pl.Element(d)),
                          lambda i: (pl.multiple_of(
                              (i // per_seq) * tp + skip + (i % per_seq) * tm, ROW_ALIGN), 0))
    assert tp % ROW_ALIGN == 0 and skip % ROW_ALIGN == 0 and tm % ROW_ALIGN == 0
    return pl.pallas_call(
        _norm_kernel,
        grid=(batch * per_seq,),
        in_specs=[x_spec, pl.BlockSpec((1, d), lambda i: (0, 0))],
        out_specs=pl.BlockSpec((tm, d), lambda i: (i, 0)),
        out_shape=jax.ShapeDtypeStruct((batch * seq, d), F32),
        compiler_params=_cparams("parallel"),
        name="final_norm",
    )(x, g)


def _pad_rows(a, height):
    return jnp.pad(a, ((0, height - a.shape[0]), (0, 0)))


def kernel(x, meta_tokens, ffn1_norm, ffn1_w_gate, ffn1_w_up, ffn1_w_down, mix_norm, w_in_first, w_in_rest, mu_shift, mu_vres, ssm_lambda_re, ssm_lambda_im, ssm_log_dt, ssm_b_re, ssm_b_im, ssm_c_re, ssm_c_im, ssm_d, ssm_w_glu, rwkv_w0, rwkv_w2, rwkv_a0, rwkv_a2, rwkv_v0, rwkv_v2, rwkv_g2, rwkv_k_k, rwkv_k_a, rwkv_r_k, rwkv_lnx_w, rwkv_lnx_b, w_up_ssm, w_up_rwkv, w_out, ffn2_norm, ffn2_w_gate, ffn2_w_up, ffn2_w_down, final_norm):
    bsz, seq, d = x.shape
    n_meta = meta_tokens.shape[0]
    depth = ffn1_norm.shape[0]
    w = ssm_d.shape[1]
    n_groups = ssm_b_re.shape[1]
    ld, la, lg, lv = rwkv_w2.shape[1], rwkv_a2.shape[1], rwkv_g2.shape[1], rwkv_v2.shape[1]
    assert w == rwkv_w0.shape[1] and 2 * w == d and w % 128 == 0
    assert n_groups * SSM_GROUP == w and ld + la <= 128 and lg <= 256 and lv <= 128
    t_len = n_meta + seq
    tp = -(-t_len // SEQ_ALIGN) * SEQ_ALIGN
    rows = bsz * tp

    meta = jnp.broadcast_to(meta_tokens[None].astype(x.dtype), (bsz, n_meta, d))
    h_res = jnp.concatenate([meta, x, jnp.zeros((bsz, tp - t_len, d), x.dtype)], axis=1)
    h_res = h_res.reshape(rows, d)

    u_col = 2 * d
    rkv_col = u_col + w
    lora_col = rkv_col + 3 * w
    c_w = 4 * w
    c_g = c_w + ld + la
    c_ga = c_g + lg
    c_gb = c_ga + d
    p_common = c_gb + d

    ffn1_w = [a.astype(BF16) for a in (ffn1_w_gate, ffn1_w_up, ffn1_w_down)]
    ffn2_w = [a.astype(BF16) for a in (ffn2_w_gate, ffn2_w_up, ffn2_w_down)]
    merge_w = [a.astype(BF16) for a in (ssm_w_glu, w_up_ssm, w_up_rwkv, w_out)]

    w_all = jnp.concatenate([jnp.pad(w_in_first, ((0, 0), (0, lv)))[None], w_in_rest], axis=0)
    pad3 = lambda a, width: jnp.pad(a, ((0, 0), (0, 0), (0, width - a.shape[2])))
    w_in = jnp.concatenate(
        [w_all[:, :, c_ga:c_gb], w_all[:, :, c_gb:p_common], w_all[:, :, :c_w],
         pad3(w_all[:, :, c_w:c_g], 128), pad3(w_all[:, :, c_g:c_ga], 256),
         pad3(w_all[:, :, p_common:], 128)], axis=2).astype(BF16)

    v_first = None
    for i in range(depth):
        h_res = _ffn(h_res, ffn1_norm[i][None], *ffn1_w, i)

        p, u = _proj(h_res, mix_norm[i][None], w_in, i, u_col, w)

        ys = _s5_scan(u, ssm_log_dt[i], ssm_lambda_re[i], ssm_lambda_im[i],
                      ssm_c_re[i], ssm_c_im[i], ssm_b_re[i], ssm_b_im[i], batch=bsz)

        mu = mu_shift[i]
        zero_w = jnp.zeros((w,), F32)
        vecs = jnp.stack([rwkv_w0[i], rwkv_a0[i], rwkv_v0[i - 1] if i else zero_w, rwkv_k_k[i],
                          rwkv_k_a[i], rwkv_r_k[i].reshape(w), rwkv_lnx_w[i], rwkv_lnx_b[i],
                          mu[0:w], mu[w:2 * w], mu[2 * w:3 * w]])
        vecs = _pad_rows(vecs, _VEC_ROWS)
        mu_v = mu_vres[i - 1] if i else jnp.zeros((lv,), F32)
        mu_l = jnp.concatenate([jnp.pad(mu[3 * w:3 * w + ld + la], (0, 128 - ld - la)),
                                jnp.pad(mu[3 * w + ld + la:], (0, 256 - lg)),
                                jnp.pad(mu_v, (0, 128 - lv))])[None]
        w2p = _pad_rows(rwkv_w2[i], 128).astype(BF16)
        a2p = jnp.pad(rwkv_a2[i], ((ld, 128 - ld - la), (0, 0))).astype(BF16)
        g2p = _pad_rows(rwkv_g2[i], 256).astype(BF16)
        v2p = _pad_rows(rwkv_v2[i - 1], 128).astype(BF16) if i else None
        yb, v_new = _rwkv(p, v_first, vecs, mu_l, w2p, a2p, g2p, v2p,
                          batch=bsz, width=w, col0=rkv_col, lora_col=lora_col)
        if i == 0:
            v_first = v_new

        h_res = _merge(h_res, p, ys, yb, ssm_d[i][None], *merge_w, u_col=u_col, layer=i)

        h_res = _ffn(h_res, ffn2_norm[i][None], *ffn2_w, i)

    out = _final_norm(h_res, final_norm[None], batch=bsz, skip=n_meta, seq=seq)
    return out.reshape(bsz, seq, d)
```
